```python
import math
import jax, jax.numpy as jnp
from jax import lax
import numpy as np

D_MODEL = 1024
BATCH = 8
SEQ = 2048
DEPTH = 2
DEC_BATCH = 128
DEC_SEQ = 1
PAST_LEN = 16384
PAGE_SIZE = 128

N_MIXERS = 2
N_CONV_LAYERS = (DEPTH + 1) // 2
N_REC_LAYERS = DEPTH // 2
CONV_WIDTH = 3
REC_HEAD_DIM = 128
REC_HEADS = D_MODEL // REC_HEAD_DIM
CHUNK = 64
N_MEM = 256
X_HEADS = 4
X_HEAD_DIM = D_MODEL // X_HEADS
D_FF = 2816
EPS = 1e-6

kernel_name = "macaron_conv_hgrn2_memxattn_step"


def rmsnorm(x, g):
    xf = x.astype(jnp.float32)
    y = xf * lax.rsqrt(jnp.mean(xf * xf, axis=-1, keepdims=True) + EPS)
    return (y * g.astype(jnp.float32)).astype(x.dtype)


def swiglu(x, w_gate, w_up, w_down):
    return (jax.nn.silu(x @ w_gate) * (x @ w_up)) @ w_down


def short_conv_mixer(x, buf, w_in, w_conv, w_out):
    T = x.shape[1]
    b_gate, c_gate, u = jnp.split(x @ w_in, 3, axis=-1)
    v = c_gate * u
    vv = jnp.concatenate([buf.astype(v.dtype), v], axis=1)
    conv = sum(w_conv[j] * vv[:, j:j + T] for j in range(CONV_WIDTH))
    y = (b_gate * conv) @ w_out
    return y, vv[:, vv.shape[1] - (CONV_WIDTH - 1):]


def hgrn2_lower_bounds(lb_raw):
    p = jax.nn.softmax(lb_raw.astype(jnp.float32), axis=0)
    return jnp.cumsum(p, axis=0) - p[0]


def gla_chunked(q, k, logf, v, S0):
    B, T, H, K = q.shape
    V = v.shape[-1]
    C = min(CHUNK, T)
    n = -(-T // C)
    pad = n * C - T

    def to_chunks(a):
        a = jnp.pad(a, ((0, 0), (0, pad), (0, 0), (0, 0)))
        return a.reshape(B, n, C, H, a.shape[-1]).transpose(1, 0, 2, 3, 4)

    causal = jnp.tril(jnp.ones((C, C), dtype=bool))[None, :, :, None, None]

    def step(S, blk):
        qc, kc, lc, vc = blk
        b = jnp.cumsum(lc, axis=1)
        o_inter = jnp.einsum('bchk,bhkv->bchv', qc * jnp.exp(b), S)
        diff = b[:, :, None] - b[:, None, :]
        decay = jnp.where(causal, jnp.exp(jnp.where(causal, diff, 0.0)), 0.0)
        att = jnp.einsum('bthk,btshk,bshk->bhts', qc, decay, kc)
        o_intra = jnp.einsum('bhts,bshv->bthv', att, vc)
        b_last = b[:, -1:]
        S_new = jnp.exp(b_last[:, 0])[..., None] * S + jnp.einsum('bshk,bshv->bhkv', kc * jnp.exp(b_last - b), vc)
        return S_new, o_inter + o_intra

    S, o = lax.scan(step, S0, (to_chunks(q), to_chunks(k), to_chunks(logf), to_chunks(v)))
    o = o.transpose(1, 0, 2, 3, 4).reshape(B, n * C, H, V)[:, :T]
    return o, S


def hgrn2_mixer(x, S0, lb, w_in, g_onorm, w_out):
    B, T, D = x.shape
    f32 = jnp.float32
    q, fz, i, g = jnp.split(x @ w_in, 4, axis=-1)
    lb32 = lb.astype(f32)
    logf = jnp.logaddexp(jnp.log(lb32), jnp.log1p(-lb32) + jax.nn.log_sigmoid(fz.astype(f32)))
    k = -jnp.expm1(logf)
    hs = lambda a: a.reshape(B, T, REC_HEADS, REC_HEAD_DIM)
    o, S = gla_chunked(hs(jax.nn.silu(q.astype(f32))), hs(k), hs(logf), hs(i.astype(f32)), S0.astype(f32))
    o = rmsnorm(o, g_onorm) * jax.nn.silu(hs(g.astype(f32)))
    y = o.reshape(B, T, D).astype(x.dtype) @ w_out
    return y, S.astype(S0.dtype)


def mem_kv(mem, g_mem, w_kv):
    B, N, _ = mem.shape
    k, v = jnp.split(rmsnorm(mem, g_mem) @ w_kv, 2, axis=-1)
    return k.reshape(B, N, X_HEADS, X_HEAD_DIM), v.reshape(B, N, X_HEADS, X_HEAD_DIM)


def cross_attn(x, mk, mv, w_q, w_o):
    B, T, D = x.shape
    q = (x @ w_q).reshape(B, T, X_HEADS, X_HEAD_DIM)
    s = jnp.einsum('bthd,bnhd->bhtn', q, mk.astype(q.dtype)).astype(jnp.float32) / math.sqrt(X_HEAD_DIM)
    p = jax.nn.softmax(s, axis=-1).astype(x.dtype)
    o = jnp.einsum('bhtn,bnhd->bthd', p, mv.astype(x.dtype)).reshape(B, T, D)
    return o @ w_o


def setup_inputs(seed: int = 0) -> dict:
    key = jax.random.key(seed)
    ks = iter(jax.random.split(key, 40))
    nrm = lambda shape, scale: jax.random.normal(next(ks), shape, jnp.float32) * scale
    gain = lambda shape: 1.0 + 0.05 * jax.random.normal(next(ks), shape, jnp.float32)
    D = D_MODEL
    return {
        "x_prompt": nrm((BATCH, SEQ, D), 1.0),
        "x_sample": nrm((DEC_BATCH, DEC_SEQ, D), 1.0),
        "mem_prompt": nrm((BATCH, N_MEM, D), 1.0),
        "state_conv": nrm((N_CONV_LAYERS, DEC_BATCH, CONV_WIDTH - 1, D), 1.0),
        "state_rec": nrm((N_REC_LAYERS, DEC_BATCH, REC_HEADS, REC_HEAD_DIM, REC_HEAD_DIM), 0.3),
        "cache_mem_k": nrm((DEPTH, DEC_BATCH, N_MEM, X_HEADS, X_HEAD_DIM), 1.0),
        "cache_mem_v": nrm((DEPTH, DEC_BATCH, N_MEM, X_HEADS, X_HEAD_DIM), 1.0),
        "norm_ffn1": gain((DEPTH, D)),
        "w_ffn1_gate": nrm((DEPTH, D, D_FF), D ** -0.5),
        "w_ffn1_up": nrm((DEPTH, D, D_FF), D ** -0.5),
        "w_ffn1_down": nrm((DEPTH, D_FF, D), D_FF ** -0.5),
        "norm_mix": gain((DEPTH, D)),
        "w_conv_in": nrm((N_CONV_LAYERS, D, 3 * D), D ** -0.5),
        "w_conv": nrm((N_CONV_LAYERS, CONV_WIDTH, D), CONV_WIDTH ** -0.5),
        "w_conv_out": nrm((N_CONV_LAYERS, D, D), D ** -0.5),
        "lb_raw": nrm((DEPTH, D), 0.1),
        "w_rec_in": nrm((N_REC_LAYERS, D, 4 * D), D ** -0.5),
        "g_rec_onorm": gain((N_REC_LAYERS, REC_HEAD_DIM)),
        "w_rec_out": nrm((N_REC_LAYERS, D, D), D ** -0.5),
        "norm_xattn": gain((DEPTH, D)),
        "norm_mem": gain((DEPTH, D)),
        "w_xq": nrm((DEPTH, D, D), D ** -0.5),
        "w_xkv": nrm((DEPTH, D, 2 * D), D ** -0.5),
        "w_xo": nrm((DEPTH, D, D), D ** -0.5),
        "norm_ffn2": gain((DEPTH, D)),
        "w_ffn2_gate": nrm((DEPTH, D, D_FF), D ** -0.5),
        "w_ffn2_up": nrm((DEPTH, D, D_FF), D ** -0.5),
        "w_ffn2_down": nrm((DEPTH, D_FF, D), D_FF ** -0.5),
        "norm_final": gain((D,)),
    }


def reference(x_prompt, x_sample, mem_prompt, state_conv, state_rec, cache_mem_k, cache_mem_v,
              norm_ffn1, w_ffn1_gate, w_ffn1_up, w_ffn1_down, norm_mix,
              w_conv_in, w_conv, w_conv_out, lb_raw, w_rec_in, g_rec_onorm, w_rec_out,
              norm_xattn, norm_mem, w_xq, w_xkv, w_xo,
              norm_ffn2, w_ffn2_gate, w_ffn2_up, w_ffn2_down, norm_final):
    lower_bounds = hgrn2_lower_bounds(lb_raw)

    def trunk(x, conv_state, rec_state, mem_k, mem_v):
        new_conv, new_rec = [], []
        for i in range(DEPTH):
            h = rmsnorm(x, norm_ffn1[i])
            x = x + 0.5 * swiglu(h, w_ffn1_gate[i], w_ffn1_up[i], w_ffn1_down[i])
            h = rmsnorm(x, norm_mix[i])
            j = i // N_MIXERS
            if i % N_MIXERS == 0:
                y, buf = short_conv_mixer(h, conv_state[j], w_conv_in[j], w_conv[j], w_conv_out[j])
                new_conv.append(buf)
            else:
                y, S = hgrn2_mixer(h, rec_state[j], lower_bounds[i], w_rec_in[j], g_rec_onorm[j], w_rec_out[j])
                new_rec.append(S)
            x = x + y
            h = rmsnorm(x, norm_xattn[i])
            x = x + cross_attn(h, mem_k[i], mem_v[i], w_xq[i], w_xo[i])
            h = rmsnorm(x, norm_ffn2[i])
            x = x + 0.5 * swiglu(h, w_ffn2_gate[i], w_ffn2_up[i], w_ffn2_down[i])
        return rmsnorm(x, norm_final), jnp.stack(new_conv), jnp.stack(new_rec)

    B = x_prompt.shape[0]
    kv_p = [mem_kv(mem_prompt, norm_mem[i], w_xkv[i]) for i in range(DEPTH)]
    mem_k_p = jnp.stack([kv[0] for kv in kv_p])
    mem_v_p = jnp.stack([kv[1] for kv in kv_p])
    conv0 = jnp.zeros((N_CONV_LAYERS, B, CONV_WIDTH - 1, D_MODEL), x_prompt.dtype)
    rec0 = jnp.zeros((N_REC_LAYERS, B, REC_HEADS, REC_HEAD_DIM, REC_HEAD_DIM), x_prompt.dtype)
    y_prompt, conv_p, rec_p = trunk(x_prompt, conv0, rec0, mem_k_p, mem_v_p)

    y_sample, conv_s, rec_s = trunk(x_sample, state_conv, state_rec, cache_mem_k, cache_mem_v)

    return (y_prompt, y_sample, mem_k_p, mem_v_p, conv_p, rec_p, conv_s, rec_s)
```

```python
import functools
import math

import jax
import jax.numpy as jnp
from jax import lax
from jax.experimental import pallas as pl
from jax.experimental.pallas import tpu as pltpu

F32 = jnp.float32
BF16 = jnp.bfloat16
EPS = 1e-6

REC_HEAD_DIM = 128
X_HEADS = 4
N_MIXERS = 2
SUBLANES = 8
LANES = 128
GLA_CHUNK = 128
VMEM_LIMIT_BYTES = 56 * 1024 * 1024

NT_DIMS = (((1,), (1,)), ((), ()))
TN_DIMS = (((0,), (0,)), ((), ()))


def _params(*sem):
    return pltpu.CompilerParams(dimension_semantics=sem, vmem_limit_bytes=VMEM_LIMIT_BYTES)


def _const_spec(shape):
    zeros = (0,) * len(shape)
    return pl.BlockSpec(shape, lambda *_: zeros, pipeline_mode=pl.Buffered(1))


def _rms(x, g):
    return x * lax.rsqrt(jnp.mean(x * x, axis=-1, keepdims=True) + EPS) * g


def _silu(x):
    return x * jax.nn.sigmoid(x)


def _dot(a, b):
    return jnp.dot(a, b, preferred_element_type=F32)


def _rms_mm_kernel(x_ref, g_ref, w_ref, *o_refs):
    h = _rms(x_ref[...], g_ref[...]).astype(BF16)
    n = o_refs[0].shape[-1]
    for j, o_ref in enumerate(o_refs):
        o_ref[...] = _dot(h, w_ref[:, j * n:(j + 1) * n])


def _rms_mm_layers(x, g, w, nsplit, bm):
    m, d = x.shape
    nl, _, n_all = w.shape
    n = n_all // nsplit
    return pl.pallas_call(
        _rms_mm_kernel,
        grid=(nl, m // bm),
        in_specs=[pl.BlockSpec((bm, d), lambda l, i: (i, 0)),
                  pl.BlockSpec((None, 1, d), lambda l, i: (l, 0, 0)),
                  pl.BlockSpec((None, d, n_all), lambda l, i: (l, 0, 0))],
        out_specs=[pl.BlockSpec((None, bm, n), lambda l, i: (l, i, 0))] * nsplit,
        out_shape=[jax.ShapeDtypeStruct((nl, m, n), F32)] * nsplit,
        compiler_params=_params("arbitrary", "arbitrary"),
        name="rms_mm",
    )(x, g.reshape(nl, 1, d), w)


def _mm_res_kernel(a_ref, w_ref, x_ref, o_ref):
    o_ref[...] = x_ref[...] + _dot(a_ref[...].astype(BF16), w_ref[...])


def _mm_res(a, w, x):
    m, d = x.shape
    return pl.pallas_call(
        _mm_res_kernel,
        grid=(1,),
        in_specs=[_const_spec(a.shape), _const_spec(w.shape), _const_spec(x.shape)],
        out_specs=pl.BlockSpec((m, d), lambda i: (0, 0)),
        out_shape=jax.ShapeDtypeStruct((m, d), F32),
        compiler_params=_params("arbitrary"),
        name="mm_res",
    )(a, w, x)


def _ffn_chunks(d_ff):
    mxu = 256
    tiles = d_ff // mxu
    if d_ff % mxu or tiles < 2:
        return ((0, d_ff),)
    first = (tiles + 1) // 2 * mxu
    return ((0, first), (first, d_ff))


def _ffn_kernel(x_ref, g_ref, wg_ref, wu_ref, wd_ref, gf_ref, o_ref, *, final_norm):
    x = x_ref[...]
    h = _rms(x, g_ref[...]).astype(BF16)
    acc = None
    for c0, c1 in _ffn_chunks(wg_ref.shape[1]):
        gate = _dot(h, wg_ref[:, c0:c1])
        up = _dot(h, wu_ref[:, c0:c1])
        part = _dot((_silu(gate) * up).astype(BF16), wd_ref[c0:c1, :])
        acc = part if acc is None else acc + part
    y = x + 0.5 * acc
    if final_norm:
        y = _rms(y, gf_ref[...])
    o_ref[...] = y


def _ffn(x, g, wg, wu, wd, g_final, final_norm, bm):
    m, d = x.shape
    return pl.pallas_call(
        functools.partial(_ffn_kernel, final_norm=final_norm),
        grid=(m // bm,),
        in_specs=[pl.BlockSpec((bm, d), lambda i: (i, 0)),
                  _const_spec((1, d)), _const_spec(wg.shape), _const_spec(wu.shape),
                  _const_spec(wd.shape), _const_spec((1, d))],
        out_specs=pl.BlockSpec((bm, d), lambda i: (i, 0)),
        out_shape=jax.ShapeDtypeStruct((m, d), F32),
        compiler_params=_params("arbitrary"),
        name="ffn",
    )(x, g.reshape(1, d), wg, wu, wd, g_final.reshape(1, d))


def _conv_prompt_kernel(x_ref, g_ref, win_ref, wc_ref, wout_ref, buf_ref, o_ref, nb_ref, vs_ref):
    tb, d = x_ref.shape

    @pl.when(pl.program_id(1) == 0)
    def _():
        vs_ref[6:8, :] = buf_ref[...]

    x = x_ref[...]
    p = _dot(_rms(x, g_ref[...]).astype(BF16), win_ref[...])
    v = p[:, d:2 * d] * p[:, 2 * d:]
    vs_ref[8:8 + tb, :] = v
    conv = (wc_ref[0:1, :] * vs_ref[6:6 + tb, :] + wc_ref[1:2, :] * vs_ref[7:7 + tb, :]
            + wc_ref[2:3, :] * v)
    o_ref[...] = x + _dot((p[:, :d] * conv).astype(BF16), wout_ref[...])
    last = vs_ref[tb + 6:tb + 8, :]
    vs_ref[6:8, :] = last
    nb_ref[...] = last


def _conv_prompt(x, g, win, wc, wout, buf, tb):
    b, t, d = x.shape
    return pl.pallas_call(
        _conv_prompt_kernel,
        grid=(b, t // tb),
        in_specs=[pl.BlockSpec((None, tb, d), lambda i, j: (i, j, 0)),
                  _const_spec((1, d)), _const_spec(win.shape), _const_spec(wc.shape),
                  _const_spec(wout.shape),
                  pl.BlockSpec((None, 2, d), lambda i, j: (i, 0, 0))],
        out_specs=[pl.BlockSpec((None, tb, d), lambda i, j: (i, j, 0)),
                   pl.BlockSpec((None, 2, d), lambda i, j: (i, 0, 0))],
        out_shape=[jax.ShapeDtypeStruct((b, t, d), F32), jax.ShapeDtypeStruct((b, 2, d), F32)],
        scratch_shapes=[pltpu.VMEM((tb + 8, d), F32)],
        compiler_params=_params("arbitrary", "arbitrary"),
        name="conv_prompt",
    )(x, g.reshape(1, d), win, wc, wout, buf)


def _conv_step_kernel(x_ref, g_ref, win_ref, wc_ref, wout_ref, buf_ref, o_ref, nb_ref):
    d = x_ref.shape[1]
    x = x_ref[...]
    p = _dot(_rms(x, g_ref[...]).astype(BF16), win_ref[...])
    v = p[:, d:2 * d] * p[:, 2 * d:]
    b0 = buf_ref[:, :d]
    b1 = buf_ref[:, d:]
    conv = wc_ref[0:1, :] * b0 + wc_ref[1:2, :] * b1 + wc_ref[2:3, :] * v
    o_ref[...] = x + _dot((p[:, :d] * conv).astype(BF16), wout_ref[...])
    nb_ref[:, :d] = b1
    nb_ref[:, d:] = v


def _conv_step(x, g, win, wc, wout, buf):
    m, d = x.shape
    return pl.pallas_call(
        _conv_step_kernel,
        grid=(1,),
        in_specs=[_const_spec((m, d)), _const_spec((1, d)), _const_spec(win.shape),
                  _const_spec(wc.shape), _const_spec(wout.shape), _const_spec((m, 2 * d))],
        out_specs=[pl.BlockSpec((m, d), lambda i: (0, 0)),
                   pl.BlockSpec((m, 2 * d), lambda i: (0, 0))],
        out_shape=[jax.ShapeDtypeStruct((m, d), F32), jax.ShapeDtypeStruct((m, 2 * d), F32)],
        compiler_params=_params("arbitrary"),
        name="conv_step",
    )(x, g.reshape(1, d), win, wc, wout, buf)


def _lower_bound(lbraw, layer):
    e = jnp.exp(lbraw - jnp.max(lbraw, axis=0, keepdims=True))
    p = e / jnp.sum(e, axis=0, keepdims=True)
    cs = p[0:1]
    for j in range(1, layer + 1):
        cs = cs + p[j:j + 1]
    return cs - p[0:1]


def _log_forget(fz, lb):
    a = jnp.log(lb)
    b = jnp.log1p(-lb) + (jnp.minimum(fz, 0.0) - jnp.log1p(jnp.exp(-jnp.abs(fz))))
    return jnp.maximum(a, b) + jnp.log1p(jnp.exp(-jnp.abs(a - b)))


def _bcast_row_of_8(x, s):
    c, d = x.shape
    x3 = x.reshape(c // SUBLANES, SUBLANES, d)
    return jnp.broadcast_to(x3[:, s:s + 1, :], x3.shape).reshape(c, d)


def _split3(x):
    hi = x.astype(BF16)
    r = x - hi.astype(F32)
    mid = r.astype(BF16)
    lo = (r - mid.astype(F32)).astype(BF16)
    return hi, mid, lo


def _gla_chunk(qs, kk, logf, v, gs, gon, st_ref):
    c, d = qs.shape
    hd = REC_HEAD_DIM
    row = lax.broadcasted_iota(jnp.int32, (c, 1), 0)
    rr = lax.broadcasted_iota(jnp.int32, (c, c), 0)
    cc = lax.broadcasted_iota(jnp.int32, (c, c), 1)

    tri = (rr >= cc).astype(BF16)
    hi, mid, lo = _split3(logf)
    b = (_dot(tri, hi) + _dot(tri, mid)) + _dot(tri, lo)

    b_last = b[c - 1:c, :]
    q_inter = (qs * jnp.exp(b)).astype(BF16)
    k_state = (kk * jnp.exp(b_last - b)).astype(BF16)
    s_decay = jnp.exp(b_last)
    vb = v.astype(BF16)

    q_lv, k_lv, masks = [], [], []
    half = c // 2
    while half >= SUBLANES:
        blk = 2 * half
        anchor = jnp.concatenate(
            [jnp.broadcast_to(b[x * blk + half - 1:x * blk + half, :], (blk, d))
             for x in range(c // blk)], axis=0)
        upper = (row % blk) >= half
        e = jnp.exp(-jnp.abs(b - anchor))
        q_lv.append(jnp.where(upper, qs * e, 0.0).astype(BF16))
        k_lv.append(jnp.where(upper, 0.0, kk * e).astype(BF16))
        masks.append((rr // blk) == (cc // blk))
        half //= 2

    rloc = row % SUBLANES
    p_diag, v_diag = [], []
    for s in range(SUBLANES):
        e = jnp.exp(jnp.minimum(b - _bcast_row_of_8(b, s), 0.0))
        p_diag.append(jnp.where(rloc >= s, qs * e * _bcast_row_of_8(kk, s), 0.0).astype(BF16))
        v_diag.append(_bcast_row_of_8(v, s))
    ones = jnp.ones((hd, hd), BF16)

    ys = []
    for h in range(d // hd):
        hs = slice(h * hd, (h + 1) * hd)
        st = st_ref[h]
        o = lax.dot_general(q_inter[:, hs], st.astype(BF16), NT_DIMS, preferred_element_type=F32)
        att = jnp.zeros((c, c), F32)
        for ql, kl, mask in zip(q_lv, k_lv, masks):
            a = lax.dot_general(ql[:, hs], kl[:, hs], NT_DIMS, preferred_element_type=F32)
            att = att + jnp.where(mask, a, 0.0)
        o = o + _dot(att.astype(BF16), vb[:, hs])
        rs = _dot(jnp.concatenate([p[:, hs] for p in p_diag], axis=0), ones)
        for s in range(SUBLANES):
            o = o + rs[s * c:(s + 1) * c] * v_diag[s][:, hs]
        st_ref[h] = st * s_decay[:, hs] + lax.dot_general(
            vb[:, hs], k_state[:, hs], TN_DIMS, preferred_element_type=F32)
        ys.append(_rms(o, gon) * gs[:, hs])
    return jnp.concatenate(ys, axis=1)


def _hgrn2_prompt_kernel(x_ref, g_ref, win_ref, lbraw_ref, gon_ref, wout_ref, s0_ref,
                         o_ref, sn_ref, st_ref, *, layer):
    tb, d = x_ref.shape
    n_heads = st_ref.shape[0]
    t = pl.program_id(1)

    @pl.when(t == 0)
    def _():
        for h in range(n_heads):
            st_ref[h] = s0_ref[h].T

    x = x_ref[...]
    p = _dot(_rms(x, g_ref[...]).astype(BF16), win_ref[...])
    logf = _log_forget(p[:, d:2 * d], _lower_bound(lbraw_ref[...], layer))
    kk = 1.0 - jnp.exp(logf)
    qs = _silu(p[:, :d])
    v = p[:, 2 * d:3 * d]
    gs = _silu(p[:, 3 * d:])
    gon = gon_ref[...]
    ys = []
    for c0 in range(0, tb, GLA_CHUNK):
        cs = slice(c0, c0 + GLA_CHUNK)
        ys.append(_gla_chunk(qs[cs], kk[cs], logf[cs], v[cs], gs[cs], gon, st_ref))
    y = jnp.concatenate(ys, axis=0)
    o_ref[...] = x + _dot(y.astype(BF16), wout_ref[...])

    @pl.when(t == pl.num_programs(1) - 1)
    def _():
        for h in range(n_heads):
            sn_ref[h] = st_ref[h].T


def _hgrn2_prompt(x, g, win, lbraw, gon, wout, s0, layer, tb):
    b, t, d = x.shape
    nh, hd = s0.shape[1], s0.shape[2]
    return pl.pallas_call(
        functools.partial(_hgrn2_prompt_kernel, layer=layer),
        grid=(b, t // tb),
        in_specs=[pl.BlockSpec((None, tb, d), lambda i, j: (i, j, 0)),
                  _const_spec((1, d)), _const_spec(win.shape), _const_spec(lbraw.shape),
                  _const_spec((1, hd)), _const_spec(wout.shape),
                  pl.BlockSpec((None, nh, hd, hd), lambda i, j: (i, 0, 0, 0))],
        out_specs=[pl.BlockSpec((None, tb, d), lambda i, j: (i, j, 0)),
                   pl.BlockSpec((None, nh, hd, hd), lambda i, j: (i, 0, 0, 0))],
        out_shape=[jax.ShapeDtypeStruct((b, t, d), F32),
                   jax.ShapeDtypeStruct(s0.shape, F32)],
        scratch_shapes=[pltpu.VMEM((nh, hd, hd), F32)],
        compiler_params=_params("arbitrary", "arbitrary"),
        name="hgrn2_prompt",
    )(x, g.reshape(1, d), win, lbraw, gon.reshape(1, hd), wout, s0)


def _hgrn2_step_kernel(p_ref, lbraw_ref, gon_ref, s_ref, y_ref, sn_ref, *, layer):
    nb = p_ref.shape[0]
    d = p_ref.shape[1] // 4
    hd = REC_HEAD_DIM
    p = p_ref[...]
    logf = _log_forget(p[:, d:2 * d], _lower_bound(lbraw_ref[...], layer))
    f = jnp.exp(logf)
    kk = 1.0 - f
    qs = _silu(p[:, :d])
    v = p[:, 2 * d:3 * d]
    gs = _silu(p[:, 3 * d:])
    gon = gon_ref[...]
    pad = jnp.zeros((hd - 3 * nb, hd), F32)
    for h in range(d // hd):
        hs = slice(h * hd, (h + 1) * hd)
        cols = jnp.concatenate([f[:, hs], kk[:, hs], qs[:, hs], pad], axis=0).T
        rows = []
        for i in range(nb):
            s_new = (cols[:, i:i + 1] * s_ref[i, h]
                     + cols[:, nb + i:nb + i + 1] * v[i:i + 1, hs])
            sn_ref[i, h] = s_new
            rows.append(jnp.sum(cols[:, 2 * nb + i:2 * nb + i + 1] * s_new, axis=0, keepdims=True))
        o = jnp.concatenate(rows, axis=0)
        y_ref[:, hs] = _rms(o, gon) * gs[:, hs]


def _hgrn2_step(p, lbraw, gon, s, rec_layer, layer, nb):
    m, d4 = p.shape
    d = d4 // 4
    nh, hd = s.shape[2], s.shape[3]
    return pl.pallas_call(
        functools.partial(_hgrn2_step_kernel, layer=layer),
        grid=(m // nb,),
        in_specs=[pl.BlockSpec((nb, d4), lambda i: (i, 0)),
                  _const_spec(lbraw.shape), _const_spec((1, hd)),
                  pl.BlockSpec((None, nb, nh, hd, hd), lambda i: (rec_layer, i, 0, 0, 0))],
        out_specs=[pl.BlockSpec((nb, d), lambda i: (i, 0)),
                   pl.BlockSpec((nb, nh, hd, hd), lambda i: (i, 0, 0, 0))],
        out_shape=[jax.ShapeDtypeStruct((m, d), F32), jax.ShapeDtypeStruct(s.shape[1:], F32)],
        compiler_params=_params("arbitrary"),
        name="hgrn2_step",
    )(p, lbraw, gon.reshape(1, hd), s)


def _softmax_rows(s):
    e = jnp.exp(s - jnp.max(s, axis=-1, keepdims=True))
    return e / jnp.sum(e, axis=-1, keepdims=True)


def _xattn_prompt_kernel(x_ref, g_ref, wq_ref, k_ref, v_ref, wo_ref, o_ref):
    d = x_ref.shape[1]
    xd = d // X_HEADS
    scale = 1.0 / math.sqrt(xd)
    x = x_ref[...]
    q = _dot(_rms(x, g_ref[...]).astype(BF16), wq_ref[...])
    outs = []
    for h in range(X_HEADS):
        hs = slice(h * xd, (h + 1) * xd)
        s = lax.dot_general(q[:, hs].astype(BF16), k_ref[:, hs].astype(BF16), NT_DIMS,
                            preferred_element_type=F32) * scale
        outs.append(_dot(_softmax_rows(s).astype(BF16), v_ref[:, hs].astype(BF16)))
    o = jnp.concatenate(outs, axis=1)
    o_ref[...] = x + _dot(o.astype(BF16), wo_ref[...])


def _xattn_prompt(x, g, wq, mk, mv, wo, layer, tq):
    b, t, d = x.shape
    n = mk.shape[2]
    return pl.pallas_call(
        _xattn_prompt_kernel,
        grid=(b, t // tq),
        in_specs=[pl.BlockSpec((None, tq, d), lambda i, j: (i, j, 0)),
                  _const_spec((1, d)), _const_spec(wq.shape),
                  pl.BlockSpec((None, None, n, d), lambda i, j: (layer, i, 0, 0)),
                  pl.BlockSpec((None, None, n, d), lambda i, j: (layer, i, 0, 0)),
                  _const_spec(wo.shape)],
        out_specs=pl.BlockSpec((None, tq, d), lambda i, j: (i, j, 0)),
        out_shape=jax.ShapeDtypeStruct((b, t, d), F32),
        compiler_params=_params("arbitrary", "arbitrary"),
        name="xattn_prompt",
    )(x, g.reshape(1, d), wq, mk, mv, wo)


def _xattn_step_kernel(q_ref, k_ref, v_ref, o_ref):
    nb, _, d = q_ref.shape
    xd = d // X_HEADS
    scale = 1.0 / math.sqrt(xd)
    head_of_lane = lax.broadcasted_iota(jnp.int32, (d, LANES), 0) // xd
    col = lax.broadcasted_iota(jnp.int32, (d, LANES), 1)
    gather = (head_of_lane == col).astype(BF16)
    head_of_lane_t = lax.broadcasted_iota(jnp.int32, (LANES, d), 1) // xd
    row_t = lax.broadcasted_iota(jnp.int32, (LANES, d), 0)
    scatter = (head_of_lane_t == row_t).astype(BF16)

    def body(i, carry):
        s = _dot((k_ref[i] * q_ref[i]).astype(BF16), gather) * scale
        e = jnp.exp(s - jnp.max(s, axis=0, keepdims=True))
        p = e / jnp.sum(e, axis=0, keepdims=True)
        pf = _dot(p.astype(BF16), scatter)
        o_ref[i] = jnp.sum(pf * v_ref[i], axis=0, keepdims=True)
        return carry

    lax.fori_loop(0, nb, body, 0)


def _xattn_step(q, ck, cv, layer, nb):
    m, d = q.shape
    n = ck.shape[2]
    out = pl.pallas_call(
        _xattn_step_kernel,
        grid=(m // nb,),
        in_specs=[pl.BlockSpec((nb, 1, d), lambda i: (i, 0, 0)),
                  pl.BlockSpec((None, nb, n, d), lambda i: (layer, i, 0, 0)),
                  pl.BlockSpec((None, nb, n, d), lambda i: (layer, i, 0, 0))],
        out_specs=pl.BlockSpec((nb, 1, d), lambda i: (i, 0, 0)),
        out_shape=jax.ShapeDtypeStruct((m, 1, d), F32),
        compiler_params=_params("arbitrary"),
        name="xattn_step",
    )(q.reshape(m, 1, d), ck, cv)
    return out.reshape(m, d)


def kernel(x_prompt, x_sample, mem_prompt, state_conv, state_rec, cache_mem_k, cache_mem_v,
           norm_ffn1, w_ffn1_gate, w_ffn1_up, w_ffn1_down, norm_mix,
           w_conv_in, w_conv, w_conv_out, lb_raw, w_rec_in, g_rec_onorm, w_rec_out,
           norm_xattn, norm_mem, w_xq, w_xkv, w_xo,
           norm_ffn2, w_ffn2_gate, w_ffn2_up, w_ffn2_down, norm_final):
    depth = norm_ffn1.shape[0]
    b, t, d = x_prompt.shape
    bs = x_sample.shape[0]
    n_mem = mem_prompt.shape[1]
    xd = d // X_HEADS
    bf = lambda w: w.astype(BF16)

    w_ffn1 = [(bf(w_ffn1_gate[i]), bf(w_ffn1_up[i]), bf(w_ffn1_down[i])) for i in range(depth)]
    w_ffn2 = [(bf(w_ffn2_gate[i]), bf(w_ffn2_up[i]), bf(w_ffn2_down[i])) for i in range(depth)]
    w_cin, w_cout = bf(w_conv_in), bf(w_conv_out)
    w_rin, w_rout = bf(w_rec_in), bf(w_rec_out)
    w_q, w_kv, w_o = bf(w_xq), bf(w_xkv), bf(w_xo)

    mem_k, mem_v = _rms_mm_layers(mem_prompt.reshape(b * n_mem, d), norm_mem, w_kv, 2, 512)
    mem_k = mem_k.reshape(depth, b, n_mem, d)
    mem_v = mem_v.reshape(depth, b, n_mem, d)

    x = x_prompt
    conv_p, rec_p = [], []
    for i in range(depth):
        x = _ffn(x.reshape(b * t, d), norm_ffn1[i], *w_ffn1[i], norm_final, False, 512)
        x = x.reshape(b, t, d)
        j = i // N_MIXERS
        if i % N_MIXERS == 0:
            x, buf = _conv_prompt(x, norm_mix[i], w_cin[j], w_conv[j], w_cout[j],
                                  jnp.zeros((b, 2, d), F32), 512)
            conv_p.append(buf)
        else:
            s0 = jnp.zeros((b,) + state_rec.shape[2:], F32)
            x, s_new = _hgrn2_prompt(x, norm_mix[i], w_rin[j], lb_raw, g_rec_onorm[j], w_rout[j],
                                     s0, i, 256)
            rec_p.append(s_new)
        x = _xattn_prompt(x, norm_xattn[i], w_q[i], mem_k, mem_v, w_o[i], i, 512)
        x = _ffn(x.reshape(b * t, d), norm_ffn2[i], *w_ffn2[i], norm_final, i == depth - 1, 512)
        x = x.reshape(b, t, d)
    y_prompt = x

    x = x_sample.reshape(bs, d)
    cache_k = cache_mem_k.reshape(depth, bs, n_mem, d)
    cache_v = cache_mem_v.reshape(depth, bs, n_mem, d)
    conv_s, rec_s = [], []
    for i in range(depth):
        x = _ffn(x, norm_ffn1[i], *w_ffn1[i], norm_final, False, bs)
        j = i // N_MIXERS
        if i % N_MIXERS == 0:
            x, buf = _conv_step(x, norm_mix[i], w_cin[j], w_conv[j], w_cout[j],
                                state_conv[j].reshape(bs, 2 * d))
            conv_s.append(buf.reshape(bs, 2, d))
        else:
            (p,) = _rms_mm_layers(x, norm_mix[i:i + 1], w_rin[j:j + 1], 1, bs)
            y, s_new = _hgrn2_step(p[0], lb_raw, g_rec_onorm[j], state_rec, j, i, SUBLANES)
            x = _mm_res(y, w_rout[j], x)
            rec_s.append(s_new)
        (q,) = _rms_mm_layers(x, norm_xattn[i:i + 1], w_q[i:i + 1], 1, bs)
        o = _xattn_step(q[0], cache_k, cache_v, i, SUBLANES)
        x = _mm_res(o, w_o[i], x)
        x = _ffn(x, norm_ffn2[i], *w_ffn2[i], norm_final, i == depth - 1, bs)
    y_sample = x.reshape(x_sample.shape)

    return (y_prompt, y_sample,
            mem_k.reshape(depth, b, n_mem, X_HEADS, xd), mem_v.reshape(depth, b, n_mem, X_HEADS, xd),
            jnp.stack(conv_p), jnp.stack(rec_p), jnp.stack(conv_s), jnp.stack(rec_s))
```

```python
import functools
import math

import jax
import jax.numpy as jnp
from jax import lax
from jax.experimental import pallas as pl
from jax.experimental.pallas import tpu as pltpu

F32 = jnp.float32
BF16 = jnp.bfloat16
EPS = 1e-6

REC_HEAD_DIM = 128
X_HEADS = 4
N_MIXERS = 2
SUBLANES = 8
LANES = 128
GLA_CHUNK = 128
VMEM_LIMIT_BYTES = 56 * 1024 * 1024

NT_DIMS = (((1,), (1,)), ((), ()))
TN_DIMS = (((0,), (0,)), ((), ()))


def _params(*sem):
    return pltpu.CompilerParams(dimension_semantics=sem, vmem_limit_bytes=VMEM_LIMIT_BYTES)


def _const_spec(shape):
    zeros = (0,) * len(shape)
    return pl.BlockSpec(shape, lambda *_: zeros, pipeline_mode=pl.Buffered(1))


def _layer_spec(stacked, layer):
    tail = stacked.shape[1:]
    zeros = (0,) * len(tail)
    return pl.BlockSpec((None,) + tail, lambda *_: (layer,) + zeros, pipeline_mode=pl.Buffered(1))


def _gains(g):
    return g.reshape(g.shape[0], 1, g.shape[1])


def _rms(x, g):
    return x * lax.rsqrt(jnp.mean(x * x, axis=-1, keepdims=True) + EPS) * g


def _silu(x):
    return x * jax.nn.sigmoid(x)


def _dot(a, b):
    return jnp.dot(a, b, preferred_element_type=F32)


def _rms_mm_kernel(x_ref, g_ref, w_ref, *o_refs):
    h = _rms(x_ref[...], g_ref[...]).astype(BF16)
    n = o_refs[0].shape[-1]
    for j, o_ref in enumerate(o_refs):
        o_ref[...] = _dot(h, w_ref[:, j * n:(j + 1) * n])


def _rms_mm_layers(x, g, g_layer, w, w_layer, nl, nsplit, bm):
    m, d = x.shape
    n_all = w.shape[2]
    n = n_all // nsplit
    return pl.pallas_call(
        _rms_mm_kernel,
        grid=(nl, m // bm),
        in_specs=[pl.BlockSpec((bm, d), lambda l, i: (i, 0)),
                  pl.BlockSpec((None, 1, d), lambda l, i: (g_layer + l, 0, 0)),
                  pl.BlockSpec((None, d, n_all), lambda l, i: (w_layer + l, 0, 0))],
        out_specs=[pl.BlockSpec((None, bm, n), lambda l, i: (l, i, 0))] * nsplit,
        out_shape=[jax.ShapeDtypeStruct((nl, m, n), F32)] * nsplit,
        compiler_params=_params("arbitrary", "arbitrary"),
        name="rms_mm",
    )(x, g, w)


def _mm_res_kernel(a_ref, w_ref, x_ref, o_ref):
    o_ref[...] = x_ref[...] + _dot(a_ref[...].astype(BF16), w_ref[...])


def _mm_res(a, w, layer, x):
    m, d = x.shape
    return pl.pallas_call(
        _mm_res_kernel,
        grid=(1,),
        in_specs=[_const_spec(a.shape), _layer_spec(w, layer), _const_spec(x.shape)],
        out_specs=pl.BlockSpec((m, d), lambda i: (0, 0)),
        out_shape=jax.ShapeDtypeStruct((m, d), F32),
        compiler_params=_params("arbitrary"),
        name="mm_res",
    )(a, w, x)


def _ffn_chunks(d_ff):
    mxu = 256
    tiles = d_ff // mxu
    if d_ff % mxu or tiles < 2:
        return ((0, d_ff),)
    first = (tiles + 1) // 2 * mxu
    return ((0, first), (first, d_ff))


def _ffn_kernel(x_ref, g_ref, wg_ref, wu_ref, wd_ref, gf_ref, o_ref, *, final_norm):
    x = x_ref[...]
    h = _rms(x, g_ref[...]).astype(BF16)
    acc = None
    for c0, c1 in _ffn_chunks(wg_ref.shape[1]):
        gate = _dot(h, wg_ref[:, c0:c1])
        up = _dot(h, wu_ref[:, c0:c1])
        part = _dot((_silu(gate) * up).astype(BF16), wd_ref[c0:c1, :])
        acc = part if acc is None else acc + part
    y = x + 0.5 * acc
    if final_norm:
        y = _rms(y, gf_ref[...])
    o_ref[...] = y


def _ffn(x, g, wg, wu, wd, layer, g_final, final_norm, bm):
    m, d = x.shape
    return pl.pallas_call(
        functools.partial(_ffn_kernel, final_norm=final_norm),
        grid=(m // bm,),
        in_specs=[pl.BlockSpec((bm, d), lambda i: (i, 0)),
                  _layer_spec(g, layer), _layer_spec(wg, layer), _layer_spec(wu, layer),
                  _layer_spec(wd, layer), _const_spec((1, d))],
        out_specs=pl.BlockSpec((bm, d), lambda i: (i, 0)),
        out_shape=jax.ShapeDtypeStruct((m, d), F32),
        compiler_params=_params("arbitrary"),
        name="ffn",
    )(x, g, wg, wu, wd, g_final.reshape(1, d))


def _conv_prompt_kernel(x_ref, g_ref, win_ref, wc_ref, wout_ref, buf_ref, o_ref, nb_ref, vs_ref):
    tb, d = x_ref.shape

    @pl.when(pl.program_id(1) == 0)
    def _():
        vs_ref[6:8, :] = buf_ref[...]

    x = x_ref[...]
    p = _dot(_rms(x, g_ref[...]).astype(BF16), win_ref[...])
    v = p[:, d:2 * d] * p[:, 2 * d:]
    vs_ref[8:8 + tb, :] = v
    conv = (wc_ref[0:1, :] * vs_ref[6:6 + tb, :] + wc_ref[1:2, :] * vs_ref[7:7 + tb, :]
            + wc_ref[2:3, :] * v)
    o_ref[...] = x + _dot((p[:, :d] * conv).astype(BF16), wout_ref[...])
    last = vs_ref[tb + 6:tb + 8, :]
    vs_ref[6:8, :] = last
    nb_ref[...] = last


def _conv_prompt(x, g, layer, win, wc, wout, mix_layer, buf, tb):
    b, t, d = x.shape
    return pl.pallas_call(
        _conv_prompt_kernel,
        grid=(b, t // tb),
        in_specs=[pl.BlockSpec((None, tb, d), lambda i, j: (i, j, 0)),
                  _layer_spec(g, layer), _layer_spec(win, mix_layer), _layer_spec(wc, mix_layer),
                  _layer_spec(wout, mix_layer),
                  pl.BlockSpec((None, 2, d), lambda i, j: (i, 0, 0))],
        out_specs=[pl.BlockSpec((None, tb, d), lambda i, j: (i, j, 0)),
                   pl.BlockSpec((None, 2, d), lambda i, j: (i, 0, 0))],
        out_shape=[jax.ShapeDtypeStruct((b, t, d), F32), jax.ShapeDtypeStruct((b, 2, d), F32)],
        scratch_shapes=[pltpu.VMEM((tb + 8, d), F32)],
        compiler_params=_params("arbitrary", "arbitrary"),
        name="conv_prompt",
    )(x, g, win, wc, wout, buf)


def _conv_step_kernel(x_ref, g_ref, win_ref, wc_ref, wout_ref, buf_ref, o_ref, nb_ref):
    d = x_ref.shape[1]
    x = x_ref[...]
    p = _dot(_rms(x, g_ref[...]).astype(BF16), win_ref[...])
    v = p[:, d:2 * d] * p[:, 2 * d:]
    b0 = buf_ref[:, :d]
    b1 = buf_ref[:, d:]
    conv = wc_ref[0:1, :] * b0 + wc_ref[1:2, :] * b1 + wc_ref[2:3, :] * v
    o_ref[...] = x + _dot((p[:, :d] * conv).astype(BF16), wout_ref[...])
    nb_ref[:, :d] = b1
    nb_ref[:, d:] = v


def _conv_step(x, g, layer, win, wc, wout, mix_layer, buf):
    m, d = x.shape
    return pl.pallas_call(
        _conv_step_kernel,
        grid=(1,),
        in_specs=[_const_spec((m, d)), _layer_spec(g, layer), _layer_spec(win, mix_layer),
                  _layer_spec(wc, mix_layer), _layer_spec(wout, mix_layer),
                  _layer_spec(buf, mix_layer)],
        out_specs=[pl.BlockSpec((m, d), lambda i: (0, 0)),
                   pl.BlockSpec((m, 2 * d), lambda i: (0, 0))],
        out_shape=[jax.ShapeDtypeStruct((m, d), F32), jax.ShapeDtypeStruct((m, 2 * d), F32)],
        compiler_params=_params("arbitrary"),
        name="conv_step",
    )(x, g, win, wc, wout, buf)


def _lower_bound(lbraw, layer):
    e = jnp.exp(lbraw - jnp.max(lbraw, axis=0, keepdims=True))
    p = e / jnp.sum(e, axis=0, keepdims=True)
    cs = p[0:1]
    for j in range(1, layer + 1):
        cs = cs + p[j:j + 1]
    return cs - p[0:1]


def _log_forget(fz, lb):
    a = jnp.log(lb)
    b = jnp.log1p(-lb) + (jnp.minimum(fz, 0.0) - jnp.log1p(jnp.exp(-jnp.abs(fz))))
    return jnp.maximum(a, b) + jnp.log1p(jnp.exp(-jnp.abs(a - b)))


def _bcast_row_of_8(x, s):
    c, d = x.shape
    x3 = x.reshape(c // SUBLANES, SUBLANES, d)
    return jnp.broadcast_to(x3[:, s:s + 1, :], x3.shape).reshape(c, d)


def _split3(x):
    hi = x.astype(BF16)
    r = x - hi.astype(F32)
    mid = r.astype(BF16)
    lo = (r - mid.astype(F32)).astype(BF16)
    return hi, mid, lo


def _gla_chunk(qs, kk, logf, v, gs, gon, st_ref):
    c, d = qs.shape
    hd = REC_HEAD_DIM
    row = lax.broadcasted_iota(jnp.int32, (c, 1), 0)
    rr = lax.broadcasted_iota(jnp.int32, (c, c), 0)
    cc = lax.broadcasted_iota(jnp.int32, (c, c), 1)

    tri = (rr >= cc).astype(BF16)
    hi, mid, lo = _split3(logf)
    b = (_dot(tri, hi) + _dot(tri, mid)) + _dot(tri, lo)

    b_last = b[c - 1:c, :]
    q_inter = (qs * jnp.exp(b)).astype(BF16)
    k_state = (kk * jnp.exp(b_last - b)).astype(BF16)
    s_decay = jnp.exp(b_last)
    vb = v.astype(BF16)

    q_lv, k_lv, masks = [], [], []
    half = c // 2
    while half >= SUBLANES:
        blk = 2 * half
        anchor = jnp.concatenate(
            [jnp.broadcast_to(b[x * blk + half - 1:x * blk + half, :], (blk, d))
             for x in range(c // blk)], axis=0)
        upper = (row % blk) >= half
        e = jnp.exp(-jnp.abs(b - anchor))
        q_lv.append(jnp.where(upper, qs * e, 0.0).astype(BF16))
        k_lv.append(jnp.where(upper, 0.0, kk * e).astype(BF16))
        masks.append((rr // blk) == (cc // blk))
        half //= 2

    rloc = row % SUBLANES
    p_diag, v_diag = [], []
    for s in range(SUBLANES):
        e = jnp.exp(jnp.minimum(b - _bcast_row_of_8(b, s), 0.0))
        p_diag.append(jnp.where(rloc >= s, qs * e * _bcast_row_of_8(kk, s), 0.0).astype(BF16))
        v_diag.append(_bcast_row_of_8(v, s))
    ones = jnp.ones((hd, hd), BF16)

    ys = []
    for h in range(d // hd):
        hs = slice(h * hd, (h + 1) * hd)
        st = st_ref[h]
        o = lax.dot_general(q_inter[:, hs], st.astype(BF16), NT_DIMS, preferred_element_type=F32)
        att = jnp.zeros((c, c), F32)
        for ql, kl, mask in zip(q_lv, k_lv, masks):
            a = lax.dot_general(ql[:, hs], kl[:, hs], NT_DIMS, preferred_element_type=F32)
            att = att + jnp.where(mask, a, 0.0)
        o = o + _dot(att.astype(BF16), vb[:, hs])
        rs = _dot(jnp.concatenate([p[:, hs] for p in p_diag], axis=0), ones)
        for s in range(SUBLANES):
            o = o + rs[s * c:(s + 1) * c] * v_diag[s][:, hs]
        st_ref[h] = st * s_decay[:, hs] + lax.dot_general(
            vb[:, hs], k_state[:, hs], TN_DIMS, preferred_element_type=F32)
        ys.append(_rms(o, gon) * gs[:, hs])
    return jnp.concatenate(ys, axis=1)


def _hgrn2_prompt_kernel(x_ref, g_ref, win_ref, lbraw_ref, gon_ref, wout_ref, s0_ref,
                         o_ref, sn_ref, st_ref, *, layer):
    tb, d = x_ref.shape
    n_heads = st_ref.shape[0]
    t = pl.program_id(1)

    @pl.when(t == 0)
    def _():
        for h in range(n_heads):
            st_ref[h] = s0_ref[h].T

    x = x_ref[...]
    p = _dot(_rms(x, g_ref[...]).astype(BF16), win_ref[...])
    logf = _log_forget(p[:, d:2 * d], _lower_bound(lbraw_ref[...], layer))
    kk = 1.0 - jnp.exp(logf)
    qs = _silu(p[:, :d])
    v = p[:, 2 * d:3 * d]
    gs = _silu(p[:, 3 * d:])
    gon = gon_ref[...]
    ys = []
    for c0 in range(0, tb, GLA_CHUNK):
        cs = slice(c0, c0 + GLA_CHUNK)
        ys.append(_gla_chunk(qs[cs], kk[cs], logf[cs], v[cs], gs[cs], gon, st_ref))
    y = jnp.concatenate(ys, axis=0)
    o_ref[...] = x + _dot(y.astype(BF16), wout_ref[...])

    @pl.when(t == pl.num_programs(1) - 1)
    def _():
        for h in range(n_heads):
            sn_ref[h] = st_ref[h].T


def _hgrn2_prompt(x, g, win, lbraw, gon, wout, s0, layer, mix_layer, tb):
    b, t, d = x.shape
    nh, hd = s0.shape[1], s0.shape[2]
    return pl.pallas_call(
        functools.partial(_hgrn2_prompt_kernel, layer=layer),
        grid=(b, t // tb),
        in_specs=[pl.BlockSpec((None, tb, d), lambda i, j: (i, j, 0)),
                  _layer_spec(g, layer), _layer_spec(win, mix_layer), _const_spec(lbraw.shape),
                  _layer_spec(gon, mix_layer), _layer_spec(wout, mix_layer),
                  pl.BlockSpec((None, nh, hd, hd), lambda i, j: (i, 0, 0, 0))],
        out_specs=[pl.BlockSpec((None, tb, d), lambda i, j: (i, j, 0)),
                   pl.BlockSpec((None, nh, hd, hd), lambda i, j: (i, 0, 0, 0))],
        out_shape=[jax.ShapeDtypeStruct((b, t, d), F32),
                   jax.ShapeDtypeStruct(s0.shape, F32)],
        scratch_shapes=[pltpu.VMEM((nh, hd, hd), F32)],
        compiler_params=_params("arbitrary", "arbitrary"),
        name="hgrn2_prompt",
    )(x, g, win, lbraw, gon, wout, s0)


def _hgrn2_step_kernel(p_ref, lbraw_ref, gon_ref, s_ref, y_ref, sn_ref, *, layer):
    nb = p_ref.shape[0]
    d = p_ref.shape[1] // 4
    hd = REC_HEAD_DIM
    p = p_ref[...]
    logf = _log_forget(p[:, d:2 * d], _lower_bound(lbraw_ref[...], layer))
    f = jnp.exp(logf)
    kk = 1.0 - f
    qs = _silu(p[:, :d])
    v = p[:, 2 * d:3 * d]
    gs = _silu(p[:, 3 * d:])
    gon = gon_ref[...]
    pad = jnp.zeros((hd - 3 * nb, hd), F32)
    for h in range(d // hd):
        hs = slice(h * hd, (h + 1) * hd)
        cols = jnp.concatenate([f[:, hs], kk[:, hs], qs[:, hs], pad], axis=0).T
        rows = []
        for i in range(nb):
            s_new = (cols[:, i:i + 1] * s_ref[i, h]
                     + cols[:, nb + i:nb + i + 1] * v[i:i + 1, hs])
            sn_ref[i, h] = s_new
            rows.append(jnp.sum(cols[:, 2 * nb + i:2 * nb + i + 1] * s_new, axis=0, keepdims=True))
        o = jnp.concatenate(rows, axis=0)
        y_ref[:, hs] = _rms(o, gon) * gs[:, hs]


def _hgrn2_step(p, lbraw, gon, s, rec_layer, layer, nb):
    m, d4 = p.shape
    d = d4 // 4
    nh, hd = s.shape[2], s.shape[3]
    return pl.pallas_call(
        functools.partial(_hgrn2_step_kernel, layer=layer),
        grid=(m // nb,),
        in_specs=[pl.BlockSpec((nb, d4), lambda i: (i, 0)),
                  _const_spec(lbraw.shape), _layer_spec(gon, rec_layer),
                  pl.BlockSpec((None, nb, nh, hd, hd), lambda i: (rec_layer, i, 0, 0, 0))],
        out_specs=[pl.BlockSpec((nb, d), lambda i: (i, 0)),
                   pl.BlockSpec((nb, nh, hd, hd), lambda i: (i, 0, 0, 0))],
        out_shape=[jax.ShapeDtypeStruct((m, d), F32), jax.ShapeDtypeStruct(s.shape[1:], F32)],
        compiler_params=_params("arbitrary"),
        name="hgrn2_step",
    )(p, lbraw, gon, s)


def _softmax_rows(s):
    e = jnp.exp(s - jnp.max(s, axis=-1, keepdims=True))
    return e / jnp.sum(e, axis=-1, keepdims=True)


def _xattn_prompt_kernel(x_ref, g_ref, wq_ref, k_ref, v_ref, wo_ref, o_ref):
    d = x_ref.shape[1]
    xd = d // X_HEADS
    scale = 1.0 / math.sqrt(xd)
    x = x_ref[...]
    q = _dot(_rms(x, g_ref[...]).astype(BF16), wq_ref[...])
    outs = []
    for h in range(X_HEADS):
        hs = slice(h * xd, (h + 1) * xd)
        s = lax.dot_general(q[:, hs].astype(BF16), k_ref[:, hs].astype(BF16), NT_DIMS,
                            preferred_element_type=F32) * scale
        outs.append(_dot(_softmax_rows(s).astype(BF16), v_ref[:, hs].astype(BF16)))
    o = jnp.concatenate(outs, axis=1)
    o_ref[...] = x + _dot(o.astype(BF16), wo_ref[...])


def _xattn_prompt(x, g, wq, mk, mv, wo, layer, tq):
    b, t, d = x.shape
    n = mk.shape[2]
    return pl.pallas_call(
        _xattn_prompt_kernel,
        grid=(b, t // tq),
        in_specs=[pl.BlockSpec((None, tq, d), lambda i, j: (i, j, 0)),
                  _layer_spec(g, layer), _layer_spec(wq, layer),
                  pl.BlockSpec((None, None, n, d), lambda i, j: (layer, i, 0, 0)),
                  pl.BlockSpec((None, None, n, d), lambda i, j: (layer, i, 0, 0)),
                  _layer_spec(wo, layer)],
        out_specs=pl.BlockSpec((None, tq, d), lambda i, j: (i, j, 0)),
        out_shape=jax.ShapeDtypeStruct((b, t, d), F32),
        compiler_params=_params("arbitrary", "arbitrary"),
        name="xattn_prompt",
    )(x, g, wq, mk, mv, wo)


def _xattn_step_kernel(q_ref, k_ref, v_ref, o_ref):
    nb, nh, xd = q_ref.shape
    scale = 1.0 / math.sqrt(xd)

    def body(i, carry):
        q = q_ref[i]
        for h in range(nh):
            s = jnp.sum(k_ref[i, :, h, :] * q[h:h + 1, :], axis=-1, keepdims=True) * scale
            e = jnp.exp(s - jnp.max(s, axis=0, keepdims=True))
            p = e / jnp.sum(e, axis=0, keepdims=True)
            o_ref[i, h:h + 1, :] = jnp.sum(p * v_ref[i, :, h, :], axis=0, keepdims=True)
        return carry

    lax.fori_loop(0, nb, body, 0)


def _xattn_step(q, ck, cv, layer, nb):
    m, d = q.shape
    _, _, n, nh, xd = ck.shape
    out = pl.pallas_call(
        _xattn_step_kernel,
        grid=(m // nb,),
        in_specs=[pl.BlockSpec((nb, nh, xd), lambda i: (i, 0, 0)),
                  pl.BlockSpec((None, nb, n, nh, xd), lambda i: (layer, i, 0, 0, 0)),
                  pl.BlockSpec((None, nb, n, nh, xd), lambda i: (layer, i, 0, 0, 0))],
        out_specs=pl.BlockSpec((nb, nh, xd), lambda i: (i, 0, 0)),
        out_shape=jax.ShapeDtypeStruct((m, nh, xd), F32),
        compiler_params=_params("arbitrary"),
        name="xattn_step",
    )(q.reshape(m, nh, xd), ck, cv)
    return out.reshape(m, d)


def kernel(x_prompt, x_sample, mem_prompt, state_conv, state_rec, cache_mem_k, cache_mem_v,
           norm_ffn1, w_ffn1_gate, w_ffn1_up, w_ffn1_down, norm_mix,
           w_conv_in, w_conv, w_conv_out, lb_raw, w_rec_in, g_rec_onorm, w_rec_out,
           norm_xattn, norm_mem, w_xq, w_xkv, w_xo,
           norm_ffn2, w_ffn2_gate, w_ffn2_up, w_ffn2_down, norm_final):
    depth = norm_ffn1.shape[0]
    b, t, d = x_prompt.shape
    bs = x_sample.shape[0]
    n_mem = mem_prompt.shape[1]
    xd = d // X_HEADS
    bf = lambda w: w.astype(BF16)

    w_ffn1 = (bf(w_ffn1_gate), bf(w_ffn1_up), bf(w_ffn1_down))
    w_ffn2 = (bf(w_ffn2_gate), bf(w_ffn2_up), bf(w_ffn2_down))
    w_cin, w_cout = bf(w_conv_in), bf(w_conv_out)
    w_rin, w_rout = bf(w_rec_in), bf(w_rec_out)
    w_q, w_kv, w_o = bf(w_xq), bf(w_xkv), bf(w_xo)
    g_ffn1, g_ffn2, g_mix = _gains(norm_ffn1), _gains(norm_ffn2), _gains(norm_mix)
    g_xattn, g_mem, g_onorm = _gains(norm_xattn), _gains(norm_mem), _gains(g_rec_onorm)

    mem_k, mem_v = _rms_mm_layers(mem_prompt.reshape(b * n_mem, d), g_mem, 0, w_kv, 0, depth, 2, 512)
    mem_k = mem_k.reshape(depth, b, n_mem, d)
    mem_v = mem_v.reshape(depth, b, n_mem, d)

    x = x_prompt
    conv_p, rec_p = [], []
    for i in range(depth):
        x = _ffn(x.reshape(b * t, d), g_ffn1, *w_ffn1, i, norm_final, False, 512)
        x = x.reshape(b, t, d)
        j = i // N_MIXERS
        if i % N_MIXERS == 0:
            x, buf = _conv_prompt(x, g_mix, i, w_cin, w_conv, w_cout, j,
                                  jnp.zeros((b, 2, d), F32), 512)
            conv_p.append(buf)
        else:
            s0 = jnp.zeros((b,) + state_rec.shape[2:], F32)
            x, s_new = _hgrn2_prompt(x, g_mix, w_rin, lb_raw, g_onorm, w_rout, s0, i, j, 256)
            rec_p.append(s_new)
        x = _xattn_prompt(x, g_xattn, w_q, mem_k, mem_v, w_o, i, 512)
        x = _ffn(x.reshape(b * t, d), g_ffn2, *w_ffn2, i, norm_final, i == depth - 1, 512)
        x = x.reshape(b, t, d)
    y_prompt = x

    x = x_sample.reshape(bs, d)
    conv_s, rec_s = [], []
    conv_bufs = state_conv.reshape(state_conv.shape[0], bs, 2 * d)
    for i in range(depth):
        x = _ffn(x, g_ffn1, *w_ffn1, i, norm_final, False, bs)
        j = i // N_MIXERS
        if i % N_MIXERS == 0:
            x, buf = _conv_step(x, g_mix, i, w_cin, w_conv, w_cout, j, conv_bufs)
            conv_s.append(buf.reshape(bs, 2, d))
        else:
            (p,) = _rms_mm_layers(x, g_mix, i, w_rin, j, 1, 1, bs)
            y, s_new = _hgrn2_step(p[0], lb_raw, g_onorm, state_rec, j, i, SUBLANES)
            x = _mm_res(y, w_rout, j, x)
            rec_s.append(s_new)
        (q,) = _rms_mm_layers(x, g_xattn, i, w_q, i, 1, 1, bs)
        o = _xattn_step(q[0], cache_mem_k, cache_mem_v, i, SUBLANES)
        x = _mm_res(o, w_o, i, x)
        x = _ffn(x, g_ffn2, *w_ffn2, i, norm_final, i == depth - 1, bs)
    y_sample = x.reshape(x_sample.shape)

    return (y_prompt, y_sample,
            mem_k.reshape(depth, b, n_mem, X_HEADS, xd), mem_v.reshape(depth, b, n_mem, X_HEADS, xd),
            jnp.stack(conv_p), jnp.stack(rec_p), jnp.stack(conv_s), jnp.stack(rec_s))
```

```python
import functools
import math

import jax
import jax.numpy as jnp
from jax import lax
from jax.experimental import pallas as pl
from jax.experimental.pallas import tpu as pltpu

F32 = jnp.float32
BF16 = jnp.bfloat16
EPS = 1e-6
LOG2_E = 1.4426950408889634

REC_HEAD_DIM = 128
X_HEADS = 4
N_MIXERS = 2
SUBLANES = 8
LANES = 128
GLA_CHUNK = 128
VMEM_LIMIT_BYTES = 56 * 1024 * 1024

NT_DIMS = (((1,), (1,)), ((), ()))
TN_DIMS = (((0,), (0,)), ((), ()))


def _params(*sem):
    return pltpu.CompilerParams(dimension_semantics=sem, vmem_limit_bytes=VMEM_LIMIT_BYTES)


def _const_spec(shape):
    zeros = (0,) * len(shape)
    return pl.BlockSpec(shape, lambda *_: zeros, pipeline_mode=pl.Buffered(1))


def _layer_spec(stacked, layer):
    tail = stacked.shape[1:]
    zeros = (0,) * len(tail)
    return pl.BlockSpec((None,) + tail, lambda *_: (layer,) + zeros, pipeline_mode=pl.Buffered(1))


def _gains(g):
    return g.reshape(g.shape[0], 1, g.shape[1])


def _rms(x, g):
    return x * lax.rsqrt(jnp.mean(x * x, axis=-1, keepdims=True) + EPS) * g


def _silu(x):
    return x * jax.nn.sigmoid(x)


def _dot(a, b):
    return jnp.dot(a, b, preferred_element_type=F32)


def _rms_mm_kernel(x_ref, g_ref, w_ref, *o_refs):
    h = _rms(x_ref[...], g_ref[...]).astype(BF16)
    n = o_refs[0].shape[-1]
    for j, o_ref in enumerate(o_refs):
        o_ref[...] = _dot(h, w_ref[:, j * n:(j + 1) * n])


def _rms_mm_layers(x, g, g_layer, w, w_layer, nl, nsplit, bm):
    m, d = x.shape
    n_all = w.shape[2]
    n = n_all // nsplit
    return pl.pallas_call(
        _rms_mm_kernel,
        grid=(nl, m // bm),
        in_specs=[pl.BlockSpec((bm, d), lambda l, i: (i, 0)),
                  pl.BlockSpec((None, 1, d), lambda l, i: (g_layer + l, 0, 0)),
                  pl.BlockSpec((None, d, n_all), lambda l, i: (w_layer + l, 0, 0))],
        out_specs=[pl.BlockSpec((None, bm, n), lambda l, i: (l, i, 0))] * nsplit,
        out_shape=[jax.ShapeDtypeStruct((nl, m, n), F32)] * nsplit,
        compiler_params=_params("arbitrary", "arbitrary"),
        name="rms_mm",
    )(x, g, w)


def _mm_res_kernel(a_ref, w_ref, x_ref, o_ref):
    o_ref[...] = x_ref[...] + _dot(a_ref[...].astype(BF16), w_ref[...])


def _mm_res(a, w, layer, x):
    m, d = x.shape
    return pl.pallas_call(
        _mm_res_kernel,
        grid=(1,),
        in_specs=[_const_spec(a.shape), _layer_spec(w, layer), _const_spec(x.shape)],
        out_specs=pl.BlockSpec((m, d), lambda i: (0, 0)),
        out_shape=jax.ShapeDtypeStruct((m, d), F32),
        compiler_params=_params("arbitrary"),
        name="mm_res",
    )(a, w, x)


def _ffn_chunks(d_ff):
    mxu = 256
    tiles = d_ff // mxu
    if d_ff % mxu or tiles < 2:
        return ((0, d_ff),)
    first = (tiles + 1) // 2 * mxu
    return ((0, first), (first, d_ff))


def _ffn_kernel(x_ref, g_ref, wg_ref, wu_ref, wd_ref, gf_ref, o_ref, *, final_norm):
    x = x_ref[...]
    h = _rms(x, g_ref[...]).astype(BF16)
    acc = None
    for c0, c1 in _ffn_chunks(wg_ref.shape[1]):
        gate = _dot(h, wg_ref[:, c0:c1])
        up = _dot(h, wu_ref[:, c0:c1])
        part = _dot((_silu(gate) * up).astype(BF16), wd_ref[c0:c1, :])
        acc = part if acc is None else acc + part
    y = x + 0.5 * acc
    if final_norm:
        y = _rms(y, gf_ref[...])
    o_ref[...] = y


def _ffn(x, g, wg, wu, wd, layer, g_final, final_norm, bm):
    m, d = x.shape
    return pl.pallas_call(
        functools.partial(_ffn_kernel, final_norm=final_norm),
        grid=(m // bm,),
        in_specs=[pl.BlockSpec((bm, d), lambda i: (i, 0)),
                  _layer_spec(g, layer), _layer_spec(wg, layer), _layer_spec(wu, layer),
                  _layer_spec(wd, layer), _const_spec((1, d))],
        out_specs=pl.BlockSpec((bm, d), lambda i: (i, 0)),
        out_shape=jax.ShapeDtypeStruct((m, d), F32),
        compiler_params=_params("arbitrary"),
        name="ffn",
    )(x, g, wg, wu, wd, g_final.reshape(1, d))


def _conv_prompt_kernel(x_ref, g_ref, win_ref, wc_ref, wout_ref, buf_ref, o_ref, nb_ref, vs_ref):
    tb, d = x_ref.shape

    @pl.when(pl.program_id(1) == 0)
    def _():
        vs_ref[6:8, :] = buf_ref[...]

    x = x_ref[...]
    p = _dot(_rms(x, g_ref[...]).astype(BF16), win_ref[...])
    v = p[:, d:2 * d] * p[:, 2 * d:]
    vs_ref[8:8 + tb, :] = v
    conv = (wc_ref[0:1, :] * vs_ref[6:6 + tb, :] + wc_ref[1:2, :] * vs_ref[7:7 + tb, :]
            + wc_ref[2:3, :] * v)
    o_ref[...] = x + _dot((p[:, :d] * conv).astype(BF16), wout_ref[...])
    last = vs_ref[tb + 6:tb + 8, :]
    vs_ref[6:8, :] = last
    nb_ref[...] = last


def _conv_prompt(x, g, layer, win, wc, wout, mix_layer, buf, tb):
    b, t, d = x.shape
    return pl.pallas_call(
        _conv_prompt_kernel,
        grid=(b, t // tb),
        in_specs=[pl.BlockSpec((None, tb, d), lambda i, j: (i, j, 0)),
                  _layer_spec(g, layer), _layer_spec(win, mix_layer), _layer_spec(wc, mix_layer),
                  _layer_spec(wout, mix_layer),
                  pl.BlockSpec((None, 2, d), lambda i, j: (i, 0, 0))],
        out_specs=[pl.BlockSpec((None, tb, d), lambda i, j: (i, j, 0)),
                   pl.BlockSpec((None, 2, d), lambda i, j: (i, 0, 0))],
        out_shape=[jax.ShapeDtypeStruct((b, t, d), F32), jax.ShapeDtypeStruct((b, 2, d), F32)],
        scratch_shapes=[pltpu.VMEM((tb + 8, d), F32)],
        compiler_params=_params("arbitrary", "arbitrary"),
        name="conv_prompt",
    )(x, g, win, wc, wout, buf)


def _conv_step_kernel(x_ref, g_ref, win_ref, wc_ref, wout_ref, buf_ref, o_ref, nb_ref):
    d = x_ref.shape[1]
    x = x_ref[...]
    p = _dot(_rms(x, g_ref[...]).astype(BF16), win_ref[...])
    v = p[:, d:2 * d] * p[:, 2 * d:]
    b0 = buf_ref[:, :d]
    b1 = buf_ref[:, d:]
    conv = wc_ref[0:1, :] * b0 + wc_ref[1:2, :] * b1 + wc_ref[2:3, :] * v
    o_ref[...] = x + _dot((p[:, :d] * conv).astype(BF16), wout_ref[...])
    nb_ref[:, :d] = b1
    nb_ref[:, d:] = v


def _conv_step(x, g, layer, win, wc, wout, mix_layer, buf):
    m, d = x.shape
    return pl.pallas_call(
        _conv_step_kernel,
        grid=(1,),
        in_specs=[_const_spec((m, d)), _layer_spec(g, layer), _layer_spec(win, mix_layer),
                  _layer_spec(wc, mix_layer), _layer_spec(wout, mix_layer),
                  _layer_spec(buf, mix_layer)],
        out_specs=[pl.BlockSpec((m, d), lambda i: (0, 0)),
                   pl.BlockSpec((m, 2 * d), lambda i: (0, 0))],
        out_shape=[jax.ShapeDtypeStruct((m, d), F32), jax.ShapeDtypeStruct((m, 2 * d), F32)],
        compiler_params=_params("arbitrary"),
        name="conv_step",
    )(x, g, win, wc, wout, buf)


def _lower_bound(lbraw, layer):
    e = jnp.exp(lbraw - jnp.max(lbraw, axis=0, keepdims=True))
    p = e / jnp.sum(e, axis=0, keepdims=True)
    cs = p[0:1]
    for j in range(1, layer + 1):
        cs = cs + p[j:j + 1]
    return cs - p[0:1]


def _log2_gates(fz, lb):
    z = fz * LOG2_E
    a = jnp.log2(lb)
    b = jnp.log1p(-lb) * LOG2_E + (jnp.minimum(z, 0.0) - jnp.log2(1.0 + jnp.exp2(-jnp.abs(z))))
    logf2 = jnp.maximum(a, b) + jnp.log2(1.0 + jnp.exp2(-jnp.abs(a - b)))
    return logf2, b - z


def _gla_chunk(qs, lk2, logf2, v, gs, gon, st_ref, side_work):
    side_work = iter(side_work)
    c, d = qs.shape
    hd = REC_HEAD_DIM
    row = lax.broadcasted_iota(jnp.int32, (c, 1), 0)
    rr = lax.broadcasted_iota(jnp.int32, (c, c), 0)
    cc = lax.broadcasted_iota(jnp.int32, (c, c), 1)
    differing_bits = rr ^ cc

    tri = (rr >= cc).astype(BF16)
    hi = logf2.astype(BF16)
    lo = (logf2 - hi.astype(F32)).astype(BF16)
    b2 = jnp.dot(jnp.concatenate([tri, tri], axis=1), jnp.concatenate([hi, lo], axis=0),
                 preferred_element_type=F32)

    b2_last = b2[c - 1:c, :]
    q_inter = (qs * jnp.exp2(b2)).astype(BF16)
    k_state = jnp.exp2((b2_last - b2) + lk2).astype(BF16)
    s_decay = jnp.exp2(b2_last)
    vb = v.astype(BF16)

    q_lv, k_lv, level_bits = [], [], []
    half = c // 2
    while half >= SUBLANES:
        blk = 2 * half
        q_parts, k_parts = [], []
        zeros = jnp.zeros((half, d), F32)
        for x in range(c // blk):
            lower = slice(x * blk, x * blk + half)
            upper = slice(x * blk + half, (x + 1) * blk)
            anchor = b2[x * blk + half - 1:x * blk + half, :]
            k_parts += [jnp.exp2((anchor - b2[lower]) + lk2[lower]), zeros]
            q_parts += [zeros, qs[upper] * jnp.exp2(b2[upper] - anchor)]
        q_lv.append(jnp.concatenate(q_parts, axis=0).astype(BF16))
        k_lv.append(jnp.concatenate(k_parts, axis=0).astype(BF16))
        level_bits.append(half)
        half //= 2
        next(side_work)()

    groups = (c // SUBLANES, SUBLANES, d)
    rloc = lax.broadcasted_iota(jnp.int32, (1, SUBLANES, 1), 1)
    b3 = b2.reshape(groups)
    u3 = (lk2 - b2).reshape(groups)
    p_diag = []
    for s in range(SUBLANES):
        not_yet = jnp.where(rloc >= s, 0.0, -1e30)
        arg = (b3 + not_yet) + u3[:, s:s + 1, :]
        p_diag.append((qs * jnp.exp2(arg.reshape(c, d))).astype(BF16))
        if s % 2:
            next(side_work)()
    sel_r = lax.broadcasted_iota(jnp.int32, (SUBLANES * hd, c), 0) // hd
    sel_c = lax.broadcasted_iota(jnp.int32, (SUBLANES * hd, c), 1) % SUBLANES
    lane_sum = (sel_r == sel_c).astype(BF16)

    heads = [slice(h * hd, (h + 1) * hd) for h in range(d // hd)]
    states = [st_ref[h] for h in range(len(heads))]
    inter, atts = [], []
    for h, hs in enumerate(heads):
        inter.append(lax.dot_general(q_inter[:, hs], states[h].astype(BF16), NT_DIMS,
                                     preferred_element_type=F32))
        att = jnp.dot(jnp.concatenate([p[:, hs] for p in p_diag], axis=1), lane_sum,
                      preferred_element_type=F32)
        for ql, kl, bit in zip(reversed(q_lv), reversed(k_lv), reversed(level_bits)):
            a = lax.dot_general(ql[:, hs], kl[:, hs], NT_DIMS, preferred_element_type=F32)
            att = jnp.where(differing_bits >= bit, a, att)
        atts.append(att.astype(BF16))
    outs = [inter[h] + jnp.dot(atts[h], vb[:, hs], preferred_element_type=F32)
            for h, hs in enumerate(heads)]
    new_states = [states[h] * s_decay[:, hs] + lax.dot_general(
        vb[:, hs], k_state[:, hs], TN_DIMS, preferred_element_type=F32)
        for h, hs in enumerate(heads)]
    ys = [_rms(outs[h], gon) * gs[:, hs] for h, hs in enumerate(heads)]
    for h in range(len(heads)):
        st_ref[h] = new_states[h]
    return jnp.concatenate(ys, axis=1)


def _hgrn2_prompt_kernel(x_ref, xn_ref, g_ref, win_ref, lbraw_ref, gon_ref, wout_ref, s0_ref,
                         o_ref, sn_ref, st_ref, p_ref, *, layer):
    tb, d = x_ref.shape
    c = GLA_CHUNK
    n_chunks = tb // c
    n_heads = st_ref.shape[0]
    ncol = 4 * d // n_heads
    t = pl.program_id(1)
    g = g_ref[...]

    def project(rows, slot, j):
        cols = slice(j * ncol, (j + 1) * ncol)
        p_ref[slot, :, cols] = _dot(rows, win_ref[:, cols])

    @pl.when(t == 0)
    def _():
        for h in range(n_heads):
            st_ref[h] = s0_ref[h].T
        first = _rms(x_ref[0:c, :], g).astype(BF16)
        for j in range(n_heads):
            project(first, 0, j)

    lb = _lower_bound(lbraw_ref[...], layer)
    gon = gon_ref[...]
    for i in range(n_chunks):
        slot = i % 2
        rows = slice(i * c, (i + 1) * c)
        nxt = x_ref[(i + 1) * c:(i + 2) * c, :] if i + 1 < n_chunks else xn_ref[...]
        nxt = _rms(nxt, g).astype(BF16)
        side_work = [functools.partial(project, nxt, 1 - slot, j) for j in range(n_heads)]
        logf2, lk2 = _log2_gates(p_ref[slot, :, d:2 * d], lb)
        y = _gla_chunk(_silu(p_ref[slot, :, :d]), lk2, logf2, p_ref[slot, :, 2 * d:3 * d],
                       _silu(p_ref[slot, :, 3 * d:]), gon, st_ref, side_work)
        o_ref[rows, :] = x_ref[rows, :] + _dot(y.astype(BF16), wout_ref[...])

    @pl.when(t == pl.num_programs(1) - 1)
    def _():
        for h in range(n_heads):
            sn_ref[h] = st_ref[h].T


def _hgrn2_prompt(x, g, win, lbraw, gon, wout, s0, layer, mix_layer, tb):
    b, t, d = x.shape
    nh, hd = s0.shape[1], s0.shape[2]
    c = GLA_CHUNK
    assert (tb // c) % 2 == 0, "projection slots alternate per chunk; a block needs an even count"
    return pl.pallas_call(
        functools.partial(_hgrn2_prompt_kernel, layer=layer),
        grid=(b, t // tb),
        in_specs=[pl.BlockSpec((None, tb, d), lambda i, j: (i, j, 0)),
                  pl.BlockSpec((None, c, d),
                               lambda i, j: (i, jnp.minimum((j + 1) * (tb // c), t // c - 1), 0)),
                  _layer_spec(g, layer), _layer_spec(win, mix_layer), _const_spec(lbraw.shape),
                  _layer_spec(gon, mix_layer), _layer_spec(wout, mix_layer),
                  pl.BlockSpec((None, nh, hd, hd), lambda i, j: (i, 0, 0, 0))],
        out_specs=[pl.BlockSpec((None, tb, d), lambda i, j: (i, j, 0)),
                   pl.BlockSpec((None, nh, hd, hd), lambda i, j: (i, 0, 0, 0))],
        out_shape=[jax.ShapeDtypeStruct((b, t, d), F32),
                   jax.ShapeDtypeStruct(s0.shape, F32)],
        scratch_shapes=[pltpu.VMEM((nh, hd, hd), F32), pltpu.VMEM((2, c, 4 * d), F32)],
        compiler_params=_params("arbitrary", "arbitrary"),
        name="hgrn2_prompt",
    )(x, x, g, win, lbraw, gon, wout, s0)


def _hgrn2_step_kernel(p_ref, lbraw_ref, gon_ref, s_ref, y_ref, sn_ref, *, layer):
    nb = p_ref.shape[0]
    d = p_ref.shape[1] // 4
    hd = REC_HEAD_DIM
    p = p_ref[...]
    logf2, lk2 = _log2_gates(p[:, d:2 * d], _lower_bound(lbraw_ref[...], layer))
    f = jnp.exp2(logf2)
    kk = jnp.exp2(lk2)
    qs = _silu(p[:, :d])
    v = p[:, 2 * d:3 * d]
    gs = _silu(p[:, 3 * d:])
    gon = gon_ref[...]
    pad = jnp.zeros((hd - 3 * nb, hd), F32)
    for h in range(d // hd):
        hs = slice(h * hd, (h + 1) * hd)
        cols = jnp.concatenate([f[:, hs], kk[:, hs], qs[:, hs], pad], axis=0).T
        rows = []
        for i in range(nb):
            s_new = (cols[:, i:i + 1] * s_ref[i, h]
                     + cols[:, nb + i:nb + i + 1] * v[i:i + 1, hs])
            sn_ref[i, h] = s_new
            rows.append(jnp.sum(cols[:, 2 * nb + i:2 * nb + i + 1] * s_new, axis=0, keepdims=True))
        o = jnp.concatenate(rows, axis=0)
        y_ref[:, hs] = _rms(o, gon) * gs[:, hs]


def _hgrn2_step(p, lbraw, gon, s, rec_layer, layer, nb):
    m, d4 = p.shape
    d = d4 // 4
    nh, hd = s.shape[2], s.shape[3]
    return pl.pallas_call(
        functools.partial(_hgrn2_step_kernel, layer=layer),
        grid=(m // nb,),
        in_specs=[pl.BlockSpec((nb, d4), lambda i: (i, 0)),
                  _const_spec(lbraw.shape), _layer_spec(gon, rec_layer),
                  pl.BlockSpec((None, nb, nh, hd, hd), lambda i: (rec_layer, i, 0, 0, 0))],
        out_specs=[pl.BlockSpec((nb, d), lambda i: (i, 0)),
                   pl.BlockSpec((nb, nh, hd, hd), lambda i: (i, 0, 0, 0))],
        out_shape=[jax.ShapeDtypeStruct((m, d), F32), jax.ShapeDtypeStruct(s.shape[1:], F32)],
        compiler_params=_params("arbitrary"),
        name="hgrn2_step",
    )(p, lbraw, gon, s)


def _softmax_rows(s):
    e = jnp.exp(s - jnp.max(s, axis=-1, keepdims=True))
    return e / jnp.sum(e, axis=-1, keepdims=True)


def _xattn_prompt_kernel(x_ref, g_ref, wq_ref, k_ref, v_ref, wo_ref, o_ref):
    d = x_ref.shape[1]
    xd = d // X_HEADS
    scale = 1.0 / math.sqrt(xd)
    x = x_ref[...]
    q = _dot(_rms(x, g_ref[...]).astype(BF16), wq_ref[...])
    outs = []
    for h in range(X_HEADS):
        hs = slice(h * xd, (h + 1) * xd)
        s = lax.dot_general(q[:, hs].astype(BF16), k_ref[:, hs].astype(BF16), NT_DIMS,
                            preferred_element_type=F32) * scale
        outs.append(_dot(_softmax_rows(s).astype(BF16), v_ref[:, hs].astype(BF16)))
    o = jnp.concatenate(outs, axis=1)
    o_ref[...] = x + _dot(o.astype(BF16), wo_ref[...])


def _xattn_prompt(x, g, wq, mk, mv, wo, layer, tq):
    b, t, d = x.shape
    n = mk.shape[2]
    return pl.pallas_call(
        _xattn_prompt_kernel,
        grid=(b, t // tq),
        in_specs=[pl.BlockSpec((None, tq, d), lambda i, j: (i, j, 0)),
                  _layer_spec(g, layer), _layer_spec(wq, layer),
                  pl.BlockSpec((None, None, n, d), lambda i, j: (layer, i, 0, 0)),
                  pl.BlockSpec((None, None, n, d), lambda i, j: (layer, i, 0, 0)),
                  _layer_spec(wo, layer)],
        out_specs=pl.BlockSpec((None, tq, d), lambda i, j: (i, j, 0)),
        out_shape=jax.ShapeDtypeStruct((b, t, d), F32),
        compiler_params=_params("arbitrary", "arbitrary"),
        name="xattn_prompt",
    )(x, g, wq, mk, mv, wo)


def _xattn_step_kernel(q_ref, k_ref, v_ref, o_ref):
    nb, nh, xd = q_ref.shape
    scale = 1.0 / math.sqrt(xd)

    def body(i, carry):
        s = jnp.sum(k_ref[i] * q_ref[i][None], axis=-1, keepdims=True) * scale
        e = jnp.exp(s - jnp.max(s, axis=0, keepdims=True))
        p = e / jnp.sum(e, axis=0, keepdims=True)
        o_ref[i] = jnp.sum(p * v_ref[i], axis=0)
        return carry

    lax.fori_loop(0, nb, body, 0, unroll=2)


def _xattn_step(q, ck, cv, layer, nb):
    m, d = q.shape
    _, _, n, nh, xd = ck.shape
    out = pl.pallas_call(
        _xattn_step_kernel,
        grid=(m // nb,),
        in_specs=[pl.BlockSpec((nb, nh, xd), lambda i: (i, 0, 0)),
                  pl.BlockSpec((None, nb, n, nh, xd), lambda i: (layer, i, 0, 0, 0)),
                  pl.BlockSpec((None, nb, n, nh, xd), lambda i: (layer, i, 0, 0, 0))],
        out_specs=pl.BlockSpec((nb, nh, xd), lambda i: (i, 0, 0)),
        out_shape=jax.ShapeDtypeStruct((m, nh, xd), F32),
        compiler_params=_params("arbitrary"),
        name="xattn_step",
    )(q.reshape(m, nh, xd), ck, cv)
    return out.reshape(m, d)


def kernel(x_prompt, x_sample, mem_prompt, state_conv, state_rec, cache_mem_k, cache_mem_v,
           norm_ffn1, w_ffn1_gate, w_ffn1_up, w_ffn1_down, norm_mix,
           w_conv_in, w_conv, w_conv_out, lb_raw, w_rec_in, g_rec_onorm, w_rec_out,
           norm_xattn, norm_mem, w_xq, w_xkv, w_xo,
           norm_ffn2, w_ffn2_gate, w_ffn2_up, w_ffn2_down, norm_final):
    depth = norm_ffn1.shape[0]
    b, t, d = x_prompt.shape
    bs = x_sample.shape[0]
    n_mem = mem_prompt.shape[1]
    xd = d // X_HEADS
    bf = lambda w: w.astype(BF16)

    w_ffn1 = (bf(w_ffn1_gate), bf(w_ffn1_up), bf(w_ffn1_down))
    w_ffn2 = (bf(w_ffn2_gate), bf(w_ffn2_up), bf(w_ffn2_down))
    w_cin, w_cout = bf(w_conv_in), bf(w_conv_out)
    w_rin, w_rout = bf(w_rec_in), bf(w_rec_out)
    w_q, w_kv, w_o = bf(w_xq), bf(w_xkv), bf(w_xo)
    g_ffn1, g_ffn2, g_mix = _gains(norm_ffn1), _gains(norm_ffn2), _gains(norm_mix)
    g_xattn, g_mem, g_onorm = _gains(norm_xattn), _gains(norm_mem), _gains(g_rec_onorm)

    mem_k, mem_v = _rms_mm_layers(mem_prompt.reshape(b * n_mem, d), g_mem, 0, w_kv, 0, depth, 2, 512)
    mem_k = mem_k.reshape(depth, b, n_mem, d)
    mem_v = mem_v.reshape(depth, b, n_mem, d)

    x = x_prompt
    conv_p, rec_p = [], []
    for i in range(depth):
        x = _ffn(x.reshape(b * t, d), g_ffn1, *w_ffn1, i, norm_final, False, 512)
        x = x.reshape(b, t, d)
        j = i // N_MIXERS
        if i % N_MIXERS == 0:
            x, buf = _conv_prompt(x, g_mix, i, w_cin, w_conv, w_cout, j,
                                  jnp.zeros((b, 2, d), F32), 512)
            conv_p.append(buf)
        else:
            s0 = jnp.zeros((b,) + state_rec.shape[2:], F32)
            x, s_new = _hgrn2_prompt(x, g_mix, w_rin, lb_raw, g_onorm, w_rout, s0, i, j, 256)
            rec_p.append(s_new)
        x = _xattn_prompt(x, g_xattn, w_q, mem_k, mem_v, w_o, i, 512)
        x = _ffn(x.reshape(b * t, d), g_ffn2, *w_ffn2, i, norm_final, i == depth - 1, 512)
        x = x.reshape(b, t, d)
    y_prompt = x

    x = x_sample.reshape(bs, d)
    conv_s, rec_s = [], []
    conv_bufs = state_conv.reshape(state_conv.shape[0], bs, 2 * d)
    for i in range(depth):
        x = _ffn(x, g_ffn1, *w_ffn1, i, norm_final, False, bs)
        j = i // N_MIXERS
        if i % N_MIXERS == 0:
            x, buf = _conv_step(x, g_mix, i, w_cin, w_conv, w_cout, j, conv_bufs)
            conv_s.append(buf.reshape(bs, 2, d))
        else:
            (p,) = _rms_mm_layers(x, g_mix, i, w_rin, j, 1, 1, bs)
            y, s_new = _hgrn2_step(p[0], lb_raw, g_onorm, state_rec, j, i, SUBLANES)
            x = _mm_res(y, w_rout, j, x)
            rec_s.append(s_new)
        (q,) = _rms_mm_layers(x, g_xattn, i, w_q, i, 1, 1, bs)
        o = _xattn_step(q[0], cache_mem_k, cache_mem_v, i, SUBLANES)
        x = _mm_res(o, w_o, i, x)
        x = _ffn(x, g_ffn2, *w_ffn2, i, norm_final, i == depth - 1, bs)
    y_sample = x.reshape(x_sample.shape)

    return (y_prompt, y_sample,
            mem_k.reshape(depth, b, n_mem, X_HEADS, xd), mem_v.reshape(depth, b, n_mem, X_HEADS, xd),
            jnp.stack(conv_p), jnp.stack(rec_p), jnp.stack(conv_s), jnp.stack(rec_s))
```

```python
import functools
import math

import jax
import jax.numpy as jnp
from jax import lax
from jax.experimental import pallas as pl
from jax.experimental.pallas import tpu as pltpu

F32 = jnp.float32
BF16 = jnp.bfloat16
EPS = 1e-6
LOG2_E = 1.4426950408889634

REC_HEAD_DIM = 128
X_HEADS = 4
N_MIXERS = 2
SUBLANES = 8
LANES = 128
GLA_CHUNK = 128
VMEM_LIMIT_BYTES = 56 * 1024 * 1024

NT_DIMS = (((1,), (1,)), ((), ()))
TN_DIMS = (((0,), (0,)), ((), ()))


def _params(*sem):
    return pltpu.CompilerParams(dimension_semantics=sem, vmem_limit_bytes=VMEM_LIMIT_BYTES)


def _const_spec(shape):
    zeros = (0,) * len(shape)
    return pl.BlockSpec(shape, lambda *_: zeros, pipeline_mode=pl.Buffered(1))


def _layer_spec(stacked, layer):
    tail = stacked.shape[1:]
    zeros = (0,) * len(tail)
    return pl.BlockSpec((None,) + tail, lambda *_: (layer,) + zeros, pipeline_mode=pl.Buffered(1))


def _gains(g):
    return g.reshape(g.shape[0], 1, g.shape[1])


def _rms(x, g):
    return x * lax.rsqrt(jnp.mean(x * x, axis=-1, keepdims=True) + EPS) * g


def _silu(x):
    return x * jax.nn.sigmoid(x)


def _dot(a, b):
    return jnp.dot(a, b, preferred_element_type=F32)


def _rms_mm_kernel(x_ref, g_ref, w_ref, *o_refs):
    h = _rms(x_ref[...], g_ref[...]).astype(BF16)
    n = o_refs[0].shape[-1]
    for j, o_ref in enumerate(o_refs):
        o_ref[...] = _dot(h, w_ref[:, j * n:(j + 1) * n])


def _rms_mm_layers(x, g, g_layer, w, w_layer, nl, nsplit, bm):
    m, d = x.shape
    n_all = w.shape[2]
    n = n_all // nsplit
    return pl.pallas_call(
        _rms_mm_kernel,
        grid=(nl, m // bm),
        in_specs=[pl.BlockSpec((bm, d), lambda l, i: (i, 0)),
                  pl.BlockSpec((None, 1, d), lambda l, i: (g_layer + l, 0, 0)),
                  pl.BlockSpec((None, d, n_all), lambda l, i: (w_layer + l, 0, 0))],
        out_specs=[pl.BlockSpec((None, bm, n), lambda l, i: (l, i, 0))] * nsplit,
        out_shape=[jax.ShapeDtypeStruct((nl, m, n), F32)] * nsplit,
        compiler_params=_params("arbitrary", "arbitrary"),
        name="rms_mm",
    )(x, g, w)


def _mem_kv_kernel(x_ref, g_ref, w_ref, k_ref, v_ref, kh_ref, vh_ref):
    d = x_ref.shape[1]
    nh, xd = kh_ref.shape[1], kh_ref.shape[2]
    kv = _dot(_rms(x_ref[...], g_ref[...]).astype(BF16), w_ref[...])
    k_ref[...] = kv[:, :d]
    v_ref[...] = kv[:, d:]
    for h in range(nh):
        kh_ref[:, h, :] = kv[:, h * xd:(h + 1) * xd]
        vh_ref[:, h, :] = kv[:, d + h * xd:d + (h + 1) * xd]


def _mem_kv(mem, g, w, nh):
    b, n, d = mem.shape
    nl = w.shape[0]
    xd = d // nh
    flat = pl.BlockSpec((None, None, n, d), lambda l, i: (l, i, 0, 0))
    per_head = pl.BlockSpec((None, None, n, nh, xd), lambda l, i: (l, i, 0, 0, 0))
    return pl.pallas_call(
        _mem_kv_kernel,
        grid=(nl, b),
        in_specs=[pl.BlockSpec((None, n, d), lambda l, i: (i, 0, 0)),
                  pl.BlockSpec((None, 1, d), lambda l, i: (l, 0, 0)),
                  pl.BlockSpec((None, d, 2 * d), lambda l, i: (l, 0, 0))],
        out_specs=[flat, flat, per_head, per_head],
        out_shape=[jax.ShapeDtypeStruct((nl, b, n, d), F32)] * 2
        + [jax.ShapeDtypeStruct((nl, b, n, nh, xd), F32)] * 2,
        compiler_params=_params("arbitrary", "arbitrary"),
        name="mem_kv",
    )(mem, g, w)


def _mm_res_kernel(a_ref, w_ref, x_ref, o_ref):
    o_ref[...] = x_ref[...] + _dot(a_ref[...].astype(BF16), w_ref[...])


def _mm_res(a, w, layer, x):
    m, d = x.shape
    return pl.pallas_call(
        _mm_res_kernel,
        grid=(1,),
        in_specs=[_const_spec(a.shape), _layer_spec(w, layer), _const_spec(x.shape)],
        out_specs=pl.BlockSpec((m, d), lambda i: (0, 0)),
        out_shape=jax.ShapeDtypeStruct((m, d), F32),
        compiler_params=_params("arbitrary"),
        name="mm_res",
    )(a, w, x)


def _ffn_chunks(d_ff):
    mxu = 256
    tiles = d_ff // mxu
    if d_ff % mxu or tiles < 2:
        return ((0, d_ff),)
    first = (tiles + 1) // 2 * mxu
    return ((0, first), (first, d_ff))


def _ffn_kernel(x_ref, g_ref, wg_ref, wu_ref, wd_ref, gf_ref, o_ref, *, final_norm):
    x = x_ref[...]
    h = _rms(x, g_ref[...]).astype(BF16)
    acc = None
    for c0, c1 in _ffn_chunks(wg_ref.shape[1]):
        gate = _dot(h, wg_ref[:, c0:c1])
        up = _dot(h, wu_ref[:, c0:c1])
        part = _dot((_silu(gate) * up).astype(BF16), wd_ref[c0:c1, :])
        acc = part if acc is None else acc + part
    y = x + 0.5 * acc
    if final_norm:
        y = _rms(y, gf_ref[...])
    o_ref[...] = y


def _ffn(x, g, wg, wu, wd, layer, g_final, final_norm, bm):
    m, d = x.shape
    return pl.pallas_call(
        functools.partial(_ffn_kernel, final_norm=final_norm),
        grid=(m // bm,),
        in_specs=[pl.BlockSpec((bm, d), lambda i: (i, 0)),
                  _layer_spec(g, layer), _layer_spec(wg, layer), _layer_spec(wu, layer),
                  _layer_spec(wd, layer), _const_spec((1, d))],
        out_specs=pl.BlockSpec((bm, d), lambda i: (i, 0)),
        out_shape=jax.ShapeDtypeStruct((m, d), F32),
        compiler_params=_params("arbitrary"),
        name="ffn",
    )(x, g, wg, wu, wd, g_final.reshape(1, d))


def _conv_prompt_kernel(x_ref, g_ref, win_ref, wc_ref, wout_ref, buf_ref, o_ref, nb_ref, vs_ref):
    tb, d = x_ref.shape

    @pl.when(pl.program_id(1) == 0)
    def _():
        vs_ref[6:8, :] = buf_ref[...]

    x = x_ref[...]
    p = _dot(_rms(x, g_ref[...]).astype(BF16), win_ref[...])
    v = p[:, d:2 * d] * p[:, 2 * d:]
    vs_ref[8:8 + tb, :] = v
    conv = (wc_ref[0:1, :] * vs_ref[6:6 + tb, :] + wc_ref[1:2, :] * vs_ref[7:7 + tb, :]
            + wc_ref[2:3, :] * v)
    o_ref[...] = x + _dot((p[:, :d] * conv).astype(BF16), wout_ref[...])
    last = vs_ref[tb + 6:tb + 8, :]
    vs_ref[6:8, :] = last
    nb_ref[...] = last


def _conv_prompt(x, g, layer, win, wc, wout, mix_layer, buf, tb):
    b, t, d = x.shape
    return pl.pallas_call(
        _conv_prompt_kernel,
        grid=(b, t // tb),
        in_specs=[pl.BlockSpec((None, tb, d), lambda i, j: (i, j, 0)),
                  _layer_spec(g, layer), _layer_spec(win, mix_layer), _layer_spec(wc, mix_layer),
                  _layer_spec(wout, mix_layer),
                  pl.BlockSpec((None, 2, d), lambda i, j: (i, 0, 0))],
        out_specs=[pl.BlockSpec((None, tb, d), lambda i, j: (i, j, 0)),
                   pl.BlockSpec((None, 2, d), lambda i, j: (i, 0, 0))],
        out_shape=[jax.ShapeDtypeStruct((b, t, d), F32), jax.ShapeDtypeStruct((b, 2, d), F32)],
        scratch_shapes=[pltpu.VMEM((tb + 8, d), F32)],
        compiler_params=_params("arbitrary", "arbitrary"),
        name="conv_prompt",
    )(x, g, win, wc, wout, buf)


def _conv_step_kernel(x_ref, g_ref, win_ref, wc_ref, wout_ref, buf_ref, o_ref, nb_ref):
    d = x_ref.shape[1]
    x = x_ref[...]
    p = _dot(_rms(x, g_ref[...]).astype(BF16), win_ref[...])
    v = p[:, d:2 * d] * p[:, 2 * d:]
    b0 = buf_ref[:, :d]
    b1 = buf_ref[:, d:]
    conv = wc_ref[0:1, :] * b0 + wc_ref[1:2, :] * b1 + wc_ref[2:3, :] * v
    o_ref[...] = x + _dot((p[:, :d] * conv).astype(BF16), wout_ref[...])
    nb_ref[:, :d] = b1
    nb_ref[:, d:] = v


def _conv_step(x, g, layer, win, wc, wout, mix_layer, buf):
    m, d = x.shape
    return pl.pallas_call(
        _conv_step_kernel,
        grid=(1,),
        in_specs=[_const_spec((m, d)), _layer_spec(g, layer), _layer_spec(win, mix_layer),
                  _layer_spec(wc, mix_layer), _layer_spec(wout, mix_layer),
                  _layer_spec(buf, mix_layer)],
        out_specs=[pl.BlockSpec((m, d), lambda i: (0, 0)),
                   pl.BlockSpec((m, 2 * d), lambda i: (0, 0))],
        out_shape=[jax.ShapeDtypeStruct((m, d), F32), jax.ShapeDtypeStruct((m, 2 * d), F32)],
        compiler_params=_params("arbitrary"),
        name="conv_step",
    )(x, g, win, wc, wout, buf)


def _lower_bound(lbraw, layer):
    e = jnp.exp(lbraw - jnp.max(lbraw, axis=0, keepdims=True))
    p = e / jnp.sum(e, axis=0, keepdims=True)
    cs = p[0:1]
    for j in range(1, layer + 1):
        cs = cs + p[j:j + 1]
    return cs - p[0:1]


def _log2_gates(fz, lb):
    z = fz * LOG2_E
    a = jnp.log2(lb)
    b = jnp.log1p(-lb) * LOG2_E + (jnp.minimum(z, 0.0) - jnp.log2(1.0 + jnp.exp2(-jnp.abs(z))))
    logf2 = jnp.maximum(a, b) + jnp.log2(1.0 + jnp.exp2(-jnp.abs(a - b)))
    return logf2, b - z


def _gla_chunk(qs, lk2, logf2, v, gs, gon, st_ref, side_work):
    side_work = iter(side_work)
    c, d = qs.shape
    hd = REC_HEAD_DIM
    row = lax.broadcasted_iota(jnp.int32, (c, 1), 0)
    rr = lax.broadcasted_iota(jnp.int32, (c, c), 0)
    cc = lax.broadcasted_iota(jnp.int32, (c, c), 1)
    differing_bits = rr ^ cc

    tri = (rr >= cc).astype(BF16)
    hi = logf2.astype(BF16)
    lo = (logf2 - hi.astype(F32)).astype(BF16)
    b2 = jnp.dot(jnp.concatenate([tri, tri], axis=1), jnp.concatenate([hi, lo], axis=0),
                 preferred_element_type=F32)

    b2_last = b2[c - 1:c, :]
    q_inter = (qs * jnp.exp2(b2)).astype(BF16)
    k_state = jnp.exp2((b2_last - b2) + lk2).astype(BF16)
    s_decay = jnp.exp2(b2_last)
    vb = v.astype(BF16)

    q_lv, k_lv, level_bits = [], [], []
    half = c // 2
    while half >= SUBLANES:
        blk = 2 * half
        q_parts, k_parts = [], []
        zeros = jnp.zeros((half, d), F32)
        for x in range(c // blk):
            lower = slice(x * blk, x * blk + half)
            upper = slice(x * blk + half, (x + 1) * blk)
            anchor = b2[x * blk + half - 1:x * blk + half, :]
            k_parts += [jnp.exp2((anchor - b2[lower]) + lk2[lower]), zeros]
            q_parts += [zeros, qs[upper] * jnp.exp2(b2[upper] - anchor)]
        q_lv.append(jnp.concatenate(q_parts, axis=0).astype(BF16))
        k_lv.append(jnp.concatenate(k_parts, axis=0).astype(BF16))
        level_bits.append(half)
        half //= 2
        next(side_work)()

    groups = (c // SUBLANES, SUBLANES, d)
    rloc = lax.broadcasted_iota(jnp.int32, (1, SUBLANES, 1), 1)
    b3 = b2.reshape(groups)
    u3 = (lk2 - b2).reshape(groups)
    p_diag = []
    for s in range(SUBLANES):
        not_yet = jnp.where(rloc >= s, 0.0, -1e30)
        arg = (b3 + not_yet) + u3[:, s:s + 1, :]
        p_diag.append((qs * jnp.exp2(arg.reshape(c, d))).astype(BF16))
        if s % 2:
            next(side_work)()
    sel_r = lax.broadcasted_iota(jnp.int32, (SUBLANES * hd, c), 0) // hd
    sel_c = lax.broadcasted_iota(jnp.int32, (SUBLANES * hd, c), 1) % SUBLANES
    lane_sum = (sel_r == sel_c).astype(BF16)

    heads = [slice(h * hd, (h + 1) * hd) for h in range(d // hd)]
    states = [st_ref[h] for h in range(len(heads))]
    inter, atts = [], []
    for h, hs in enumerate(heads):
        inter.append(lax.dot_general(q_inter[:, hs], states[h].astype(BF16), NT_DIMS,
                                     preferred_element_type=F32))
        att = jnp.dot(jnp.concatenate([p[:, hs] for p in p_diag], axis=1), lane_sum,
                      preferred_element_type=F32)
        for ql, kl, bit in zip(reversed(q_lv), reversed(k_lv), reversed(level_bits)):
            a = lax.dot_general(ql[:, hs], kl[:, hs], NT_DIMS, preferred_element_type=F32)
            att = jnp.where(differing_bits >= bit, a, att)
        atts.append(att.astype(BF16))
    outs = [inter[h] + jnp.dot(atts[h], vb[:, hs], preferred_element_type=F32)
            for h, hs in enumerate(heads)]
    new_states = [states[h] * s_decay[:, hs] + lax.dot_general(
        vb[:, hs], k_state[:, hs], TN_DIMS, preferred_element_type=F32)
        for h, hs in enumerate(heads)]
    ys = [_rms(outs[h], gon) * gs[:, hs] for h, hs in enumerate(heads)]
    for h in range(len(heads)):
        st_ref[h] = new_states[h]
    return jnp.concatenate(ys, axis=1)


def _hgrn2_prompt_kernel(x_ref, xn_ref, g_ref, win_ref, lbraw_ref, gon_ref, wout_ref, s0_ref,
                         o_ref, sn_ref, st_ref, p_ref, *, layer):
    tb, d = x_ref.shape
    c = GLA_CHUNK
    n_chunks = tb // c
    n_heads = st_ref.shape[0]
    ncol = 4 * d // n_heads
    t = pl.program_id(1)
    g = g_ref[...]

    def project(rows, slot, j):
        cols = slice(j * ncol, (j + 1) * ncol)
        p_ref[slot, :, cols] = _dot(rows, win_ref[:, cols])

    @pl.when(t == 0)
    def _():
        for h in range(n_heads):
            st_ref[h] = s0_ref[h].T
        first = _rms(x_ref[0:c, :], g).astype(BF16)
        for j in range(n_heads):
            project(first, 0, j)

    lb = _lower_bound(lbraw_ref[...], layer)
    gon = gon_ref[...]
    for i in range(n_chunks):
        slot = i % 2
        rows = slice(i * c, (i + 1) * c)
        nxt = x_ref[(i + 1) * c:(i + 2) * c, :] if i + 1 < n_chunks else xn_ref[...]
        nxt = _rms(nxt, g).astype(BF16)
        side_work = [functools.partial(project, nxt, 1 - slot, j) for j in range(n_heads)]
        logf2, lk2 = _log2_gates(p_ref[slot, :, d:2 * d], lb)
        y = _gla_chunk(_silu(p_ref[slot, :, :d]), lk2, logf2, p_ref[slot, :, 2 * d:3 * d],
                       _silu(p_ref[slot, :, 3 * d:]), gon, st_ref, side_work)
        o_ref[rows, :] = x_ref[rows, :] + _dot(y.astype(BF16), wout_ref[...])

    @pl.when(t == pl.num_programs(1) - 1)
    def _():
        for h in range(n_heads):
            sn_ref[h] = st_ref[h].T


def _hgrn2_prompt(x, g, win, lbraw, gon, wout, s0, layer, mix_layer, tb):
    b, t, d = x.shape
    nh, hd = s0.shape[1], s0.shape[2]
    c = GLA_CHUNK
    assert (tb // c) % 2 == 0, "projection slots alternate per chunk; a block needs an even count"
    return pl.pallas_call(
        functools.partial(_hgrn2_prompt_kernel, layer=layer),
        grid=(b, t // tb),
        in_specs=[pl.BlockSpec((None, tb, d), lambda i, j: (i, j, 0)),
                  pl.BlockSpec((None, c, d),
                               lambda i, j: (i, jnp.minimum((j + 1) * (tb // c), t // c - 1), 0)),
                  _layer_spec(g, layer), _layer_spec(win, mix_layer), _const_spec(lbraw.shape),
                  _layer_spec(gon, mix_layer), _layer_spec(wout, mix_layer),
                  pl.BlockSpec((None, nh, hd, hd), lambda i, j: (i, 0, 0, 0))],
        out_specs=[pl.BlockSpec((None, tb, d), lambda i, j: (i, j, 0)),
                   pl.BlockSpec((None, nh, hd, hd), lambda i, j: (i, 0, 0, 0))],
        out_shape=[jax.ShapeDtypeStruct((b, t, d), F32),
                   jax.ShapeDtypeStruct(s0.shape, F32)],
        scratch_shapes=[pltpu.VMEM((nh, hd, hd), F32), pltpu.VMEM((2, c, 4 * d), F32)],
        compiler_params=_params("arbitrary", "arbitrary"),
        name="hgrn2_prompt",
    )(x, x, g, win, lbraw, gon, wout, s0)


def _hgrn2_step_kernel(p_ref, lbraw_ref, gon_ref, s_ref, y_ref, sn_ref, *, layer):
    nb = p_ref.shape[0]
    d = p_ref.shape[1] // 4
    hd = REC_HEAD_DIM
    p = p_ref[...]
    logf2, lk2 = _log2_gates(p[:, d:2 * d], _lower_bound(lbraw_ref[...], layer))
    f = jnp.exp2(logf2)
    kk = jnp.exp2(lk2)
    qs = _silu(p[:, :d])
    v = p[:, 2 * d:3 * d]
    gs = _silu(p[:, 3 * d:])
    gon = gon_ref[...]
    pad = jnp.zeros((hd - 3 * nb, hd), F32)
    for h in range(d // hd):
        hs = slice(h * hd, (h + 1) * hd)
        cols = jnp.concatenate([f[:, hs], kk[:, hs], qs[:, hs], pad], axis=0).T
        rows = []
        for i in range(nb):
            s_new = (cols[:, i:i + 1] * s_ref[i, h]
                     + cols[:, nb + i:nb + i + 1] * v[i:i + 1, hs])
            sn_ref[i, h] = s_new
            rows.append(jnp.sum(cols[:, 2 * nb + i:2 * nb + i + 1] * s_new, axis=0, keepdims=True))
        o = jnp.concatenate(rows, axis=0)
        y_ref[:, hs] = _rms(o, gon) * gs[:, hs]


def _hgrn2_step(p, lbraw, gon, s, rec_layer, layer, nb):
    m, d4 = p.shape
    d = d4 // 4
    nh, hd = s.shape[2], s.shape[3]
    return pl.pallas_call(
        functools.partial(_hgrn2_step_kernel, layer=layer),
        grid=(m // nb,),
        in_specs=[pl.BlockSpec((nb, d4), lambda i: (i, 0)),
                  _const_spec(lbraw.shape), _layer_spec(gon, rec_layer),
                  pl.BlockSpec((None, nb, nh, hd, hd), lambda i: (rec_layer, i, 0, 0, 0))],
        out_specs=[pl.BlockSpec((nb, d), lambda i: (i, 0)),
                   pl.BlockSpec((nb, nh, hd, hd), lambda i: (i, 0, 0, 0))],
        out_shape=[jax.ShapeDtypeStruct((m, d), F32), jax.ShapeDtypeStruct(s.shape[1:], F32)],
        compiler_params=_params("arbitrary"),
        name="hgrn2_step",
    )(p, lbraw, gon, s)


def _softmax_rows(s):
    e = jnp.exp(s - jnp.max(s, axis=-1, keepdims=True))
    return e / jnp.sum(e, axis=-1, keepdims=True)


def _xattn_prompt_kernel(x_ref, g_ref, wq_ref, k_ref, v_ref, wo_ref, o_ref):
    d = x_ref.shape[1]
    xd = d // X_HEADS
    scale = 1.0 / math.sqrt(xd)
    x = x_ref[...]
    q = _dot(_rms(x, g_ref[...]).astype(BF16), wq_ref[...]).astype(BF16)
    k = k_ref[...].astype(BF16)
    v = v_ref[...].astype(BF16)
    heads = [slice(h * xd, (h + 1) * xd) for h in range(X_HEADS)]
    scores = [lax.dot_general(q[:, hs], k[:, hs], NT_DIMS, preferred_element_type=F32) * scale
              for hs in heads]
    probs = [_softmax_rows(s).astype(BF16) for s in scores]
    o = jnp.concatenate([_dot(p, v[:, hs]) for p, hs in zip(probs, heads)], axis=1)
    o_ref[...] = x + _dot(o.astype(BF16), wo_ref[...])


def _xattn_prompt(x, g, wq, mk, mv, wo, layer, tq):
    b, t, d = x.shape
    n = mk.shape[2]
    return pl.pallas_call(
        _xattn_prompt_kernel,
        grid=(b, t // tq),
        in_specs=[pl.BlockSpec((None, tq, d), lambda i, j: (i, j, 0)),
                  _layer_spec(g, layer), _layer_spec(wq, layer),
                  pl.BlockSpec((None, None, n, d), lambda i, j: (layer, i, 0, 0)),
                  pl.BlockSpec((None, None, n, d), lambda i, j: (layer, i, 0, 0)),
                  _layer_spec(wo, layer)],
        out_specs=pl.BlockSpec((None, tq, d), lambda i, j: (i, j, 0)),
        out_shape=jax.ShapeDtypeStruct((b, t, d), F32),
        compiler_params=_params("arbitrary", "arbitrary"),
        name="xattn_prompt",
    )(x, g, wq, mk, mv, wo)


def _xattn_step_kernel(q_ref, k_ref, v_ref, o_ref):
    nb, nh, xd = q_ref.shape
    scale = 1.0 / math.sqrt(xd)

    def body(i, carry):
        s = jnp.sum(k_ref[i] * q_ref[i][None], axis=-1, keepdims=True) * scale
        e = jnp.exp(s - jnp.max(s, axis=0, keepdims=True))
        p = e / jnp.sum(e, axis=0, keepdims=True)
        o_ref[i] = jnp.sum(p * v_ref[i], axis=0)
        return carry

    lax.fori_loop(0, nb, body, 0, unroll=2)


def _xattn_step(q, ck, cv, layer, nb):
    m, d = q.shape
    _, _, n, nh, xd = ck.shape
    out = pl.pallas_call(
        _xattn_step_kernel,
        grid=(m // nb,),
        in_specs=[pl.BlockSpec((nb, nh, xd), lambda i: (i, 0, 0)),
                  pl.BlockSpec((None, nb, n, nh, xd), lambda i: (layer, i, 0, 0, 0)),
                  pl.BlockSpec((None, nb, n, nh, xd), lambda i: (layer, i, 0, 0, 0))],
        out_specs=pl.BlockSpec((nb, nh, xd), lambda i: (i, 0, 0)),
        out_shape=jax.ShapeDtypeStruct((m, nh, xd), F32),
        compiler_params=_params("arbitrary"),
        name="xattn_step",
    )(q.reshape(m, nh, xd), ck, cv)
    return out.reshape(m, d)


def kernel(x_prompt, x_sample, mem_prompt, state_conv, state_rec, cache_mem_k, cache_mem_v,
           norm_ffn1, w_ffn1_gate, w_ffn1_up, w_ffn1_down, norm_mix,
           w_conv_in, w_conv, w_conv_out, lb_raw, w_rec_in, g_rec_onorm, w_rec_out,
           norm_xattn, norm_mem, w_xq, w_xkv, w_xo,
           norm_ffn2, w_ffn2_gate, w_ffn2_up, w_ffn2_down, norm_final):
    depth = norm_ffn1.shape[0]
    b, t, d = x_prompt.shape
    bs = x_sample.shape[0]
    bf = lambda w: w.astype(BF16)

    w_ffn1 = (bf(w_ffn1_gate), bf(w_ffn1_up), bf(w_ffn1_down))
    w_ffn2 = (bf(w_ffn2_gate), bf(w_ffn2_up), bf(w_ffn2_down))
    w_cin, w_cout = bf(w_conv_in), bf(w_conv_out)
    w_rin, w_rout = bf(w_rec_in), bf(w_rec_out)
    w_q, w_kv, w_o = bf(w_xq), bf(w_xkv), bf(w_xo)
    g_ffn1, g_ffn2, g_mix = _gains(norm_ffn1), _gains(norm_ffn2), _gains(norm_mix)
    g_xattn, g_mem, g_onorm = _gains(norm_xattn), _gains(norm_mem), _gains(g_rec_onorm)

    mem_k, mem_v, mem_k_heads, mem_v_heads = _mem_kv(mem_prompt, g_mem, w_kv, X_HEADS)

    x = x_prompt
    conv_p, rec_p = [], []
    for i in range(depth):
        x = _ffn(x.reshape(b * t, d), g_ffn1, *w_ffn1, i, norm_final, False, 512)
        x = x.reshape(b, t, d)
        j = i // N_MIXERS
        if i % N_MIXERS == 0:
            x, buf = _conv_prompt(x, g_mix, i, w_cin, w_conv, w_cout, j,
                                  jnp.zeros((b, 2, d), F32), 512)
            conv_p.append(buf)
        else:
            s0 = jnp.zeros((b,) + state_rec.shape[2:], F32)
            x, s_new = _hgrn2_prompt(x, g_mix, w_rin, lb_raw, g_onorm, w_rout, s0, i, j, 256)
            rec_p.append(s_new)
        x = _xattn_prompt(x, g_xattn, w_q, mem_k, mem_v, w_o, i, 512)
        x = _ffn(x.reshape(b * t, d), g_ffn2, *w_ffn2, i, norm_final, i == depth - 1, 512)
        x = x.reshape(b, t, d)
    y_prompt = x

    x = x_sample.reshape(bs, d)
    conv_s, rec_s = [], []
    conv_bufs = state_conv.reshape(state_conv.shape[0], bs, 2 * d)
    for i in range(depth):
        x = _ffn(x, g_ffn1, *w_ffn1, i, norm_final, False, bs)
        j = i // N_MIXERS
        if i % N_MIXERS == 0:
            x, buf = _conv_step(x, g_mix, i, w_cin, w_conv, w_cout, j, conv_bufs)
            conv_s.append(buf.reshape(bs, 2, d))
        else:
            (p,) = _rms_mm_layers(x, g_mix, i, w_rin, j, 1, 1, bs)
            y, s_new = _hgrn2_step(p[0], lb_raw, g_onorm, state_rec, j, i, SUBLANES)
            x = _mm_res(y, w_rout, j, x)
            rec_s.append(s_new)
        (q,) = _rms_mm_layers(x, g_xattn, i, w_q, i, 1, 1, bs)
        o = _xattn_step(q[0], cache_mem_k, cache_mem_v, i, SUBLANES)
        x = _mm_res(o, w_o, i, x)
        x = _ffn(x, g_ffn2, *w_ffn2, i, norm_final, i == depth - 1, bs)
    y_sample = x.reshape(x_sample.shape)

    return (y_prompt, y_sample, mem_k_heads, mem_v_heads,
            jnp.stack(conv_p), jnp.stack(rec_p), jnp.stack(conv_s), jnp.stack(rec_s))
```
